```python
import math
import jax, jax.numpy as jnp
from jax import lax
import numpy as np

D_MODEL = 1024
BATCH = 4
SEQ = 8192
DEPTH = 4

CHUNK = 64
Q_BLOCK = 128
N_MIXERS = 2
BRANCH_WIDTH = D_MODEL
IN_WIDTH = 4 * BRANCH_WIDTH
SB_HEADS = 16
SB_HEAD_DIM = BRANCH_WIDTH // SB_HEADS
DIFF_HEADS = 8
DIFF_HEAD_DIM = BRANCH_WIDTH // (2 * DIFF_HEADS)
N_DIFF_LAYERS = DEPTH // N_MIXERS
ALIBI_MAX_EXP = 8.0
RMS_EPS = 1e-6
NEG_INF = -1e30

kernel_name = "hybrid_stickbreak_diffattn_adaln_trunk"


def rms_norm(x, g):
    xf = x.astype(jnp.float32)
    y = xf * lax.rsqrt(jnp.mean(xf * xf, axis=-1, keepdims=True) + RMS_EPS)
    return (y * g.astype(jnp.float32)).astype(x.dtype)


def split_query_blocks(t):
    s, e = t.shape[-2], t.shape[-1]
    t = t.reshape(t.shape[:-2] + (s // Q_BLOCK, Q_BLOCK, e))
    return jnp.moveaxis(t, -3, 0)


def merge_query_blocks(t):
    nb, b, h, q, e = t.shape
    t = jnp.moveaxis(t, 0, 2).reshape(b, h, nb * q, e)
    return t.transpose(0, 2, 1, 3).reshape(b, nb * q, h * e)


def alibi_slopes(n_heads):
    exps = jnp.arange(1, n_heads + 1, dtype=jnp.float32) * (ALIBI_MAX_EXP / n_heads)
    return jnp.exp2(-exps)


def stick_breaking_attention(qkv):
    b, s, _ = qkv.shape
    q, k, v = jnp.split(qkv, 3, axis=-1)
    heads = lambda t: t.reshape(b, s, SB_HEADS, SB_HEAD_DIM).transpose(0, 2, 1, 3)
    q, k, v = heads(q), heads(k), heads(v)
    key_pos = jnp.arange(s)
    scale = SB_HEAD_DIM ** -0.5

    def block(args):
        qb, blk = args
        q_pos = blk * Q_BLOCK + jnp.arange(Q_BLOCK)
        strict = key_pos[None, :] < q_pos[:, None]
        logits = jnp.einsum('bhqd,bhsd->bhqs', qb, k).astype(jnp.float32) * scale
        log_beta = jax.nn.log_sigmoid(logits)
        log_keep = jnp.where(strict, log_beta - logits, 0.0)
        between = lax.cumsum(log_keep, axis=3, reverse=True) - log_keep
        weights = jnp.where(strict, jnp.exp(log_beta + between), 0.0)
        return jnp.einsum('bhqs,bhsd->bhqd', weights.astype(v.dtype), v)

    out = lax.map(block, (split_query_blocks(q), jnp.arange(s // Q_BLOCK)))
    return merge_query_blocks(out)


def diff_attention(qkv, lam, g_sub, lam_init):
    b, s, _ = qkv.shape
    q, k, v = jnp.split(qkv, 3, axis=-1)
    qk_heads = lambda t: t.reshape(b, s, DIFF_HEADS, 2, DIFF_HEAD_DIM).transpose(3, 0, 2, 1, 4)
    q, k = qk_heads(q), qk_heads(k)
    v = v.reshape(b, s, DIFF_HEADS, 2 * DIFF_HEAD_DIM).transpose(0, 2, 1, 3)
    lamf = lam.astype(jnp.float32)
    lam_full = jnp.exp(jnp.sum(lamf[0] * lamf[1])) - jnp.exp(jnp.sum(lamf[2] * lamf[3])) + lam_init
    slopes = alibi_slopes(DIFF_HEADS)
    key_pos = jnp.arange(s)
    key_chunk = key_pos // CHUNK
    scale = DIFF_HEAD_DIM ** -0.5

    def block(args):
        qb, blk = args
        q_pos = blk * Q_BLOCK + jnp.arange(Q_BLOCK)
        visible = key_chunk[None, :] <= (q_pos // CHUNK)[:, None]
        dist = jnp.abs(q_pos[:, None] - key_pos[None, :]).astype(jnp.float32)
        bias = jnp.where(visible[None], -slopes[:, None, None] * dist[None], NEG_INF)
        logits = jnp.einsum('nbhqd,nbhsd->nbhqs', qb, k).astype(jnp.float32) * scale + bias[None, None]
        probs = jax.nn.softmax(logits, axis=-1)
        weights = probs[0] - lam_full * probs[1]
        return jnp.einsum('bhqs,bhse->bhqe', weights.astype(v.dtype), v)

    out = lax.map(block, (split_query_blocks(q), jnp.arange(s // Q_BLOCK)))
    out = merge_query_blocks(out).reshape(b, s, DIFF_HEADS, 2 * DIFF_HEAD_DIM)
    out = rms_norm(out, g_sub) * (1.0 - lam_init)
    return out.reshape(b, s, BRANCH_WIDTH)


def diff_lambda_init(layer_idx):
    return 0.8 - 0.6 * math.exp(-0.3 * layer_idx)


def setup_inputs(seed: int = 0) -> dict:
    key = jax.random.key(seed)
    ks = jax.random.split(key, 12)
    d = D_MODEL
    f = jnp.float32
    x = jax.random.normal(ks[0], (BATCH, SEQ, d), f)
    c = jax.random.normal(ks[1], (BATCH, d), f)
    w_ada = jax.random.normal(ks[2], (DEPTH, d, 3 * d), f) * (0.5 * d ** -0.5)
    b_ada = jax.random.normal(ks[3], (DEPTH, 3 * d), f) * 0.01
    g_pre = 1.0 + 0.1 * jax.random.normal(ks[4], (DEPTH, d), f)
    g_post = 1.0 + 0.1 * jax.random.normal(ks[5], (DEPTH, d), f)
    w_in = jax.random.normal(ks[6], (DEPTH, d, IN_WIDTH), f) * d ** -0.5
    w_out = jax.random.normal(ks[7], (DEPTH, BRANCH_WIDTH, d), f) * BRANCH_WIDTH ** -0.5
    diff_lambda = 0.1 * jax.random.normal(ks[8], (N_DIFF_LAYERS, 4, DIFF_HEAD_DIM), f)
    diff_subln = 1.0 + 0.1 * jax.random.normal(ks[9], (N_DIFF_LAYERS, 2 * DIFF_HEAD_DIM), f)
    return {"x": x, "c": c, "w_ada": w_ada, "b_ada": b_ada, "g_pre": g_pre, "g_post": g_post,
            "w_in": w_in, "w_out": w_out, "diff_lambda": diff_lambda, "diff_subln": diff_subln}


def reference(x, c, w_ada, b_ada, g_pre, g_post, w_in, w_out, diff_lambda, diff_subln):
    cond = jax.nn.silu(c)
    for i in range(DEPTH):
        ada = cond @ w_ada[i] + b_ada[i]
        shift, scale, gate = jnp.split(ada[:, None, :], 3, axis=-1)
        h = rms_norm(x, g_pre[i]) * (1.0 + scale) + shift
        u = h @ w_in[i]
        mix_in, z = u[..., :3 * BRANCH_WIDTH], u[..., 3 * BRANCH_WIDTH:]
        if i % N_MIXERS == 0:
            y = stick_breaking_attention(mix_in)
        else:
            j = i // N_MIXERS
            y = diff_attention(mix_in, diff_lambda[j], diff_subln[j], diff_lambda_init(i))
        y = (y * jax.nn.silu(z)) @ w_out[i]
        x = x + gate * rms_norm(y, g_post[i])
    return x
```

```python
import functools
import math

import jax
import jax.numpy as jnp
from jax import lax
from jax.experimental import pallas as pl
from jax.experimental.pallas import tpu as pltpu

F32 = jnp.float32
BF16 = jnp.bfloat16

LANES = 128
SUBLANES = 8
N_MIXERS = 2
CHUNK = 64
SB_HEAD_DIM = 64
DIFF_HEAD_DIM = 64
ALIBI_MAX_EXP = 8.0
RMS_EPS = 1e-6
NEG_INF = -1e30

ROW_TILE = 512
ATT_TILE = 256
VMEM_LIMIT = 48 * 1024 * 1024

_NT = (((1,), (1,)), ((), ()))


def _diff_lambda_init(layer_idx):
    return 0.8 - 0.6 * math.exp(-0.3 * layer_idx)


def _ada_kernel(c_ref, w_ref, b_ref, o_ref):
    c = c_ref[...]
    cond = c * jax.nn.sigmoid(c)
    o_ref[0] = jnp.dot(cond, w_ref[0], preferred_element_type=F32,
                       precision=lax.Precision.HIGHEST) + b_ref[0]


def _ada(c, w_ada, b_ada):
    depth, d, d3 = w_ada.shape
    nb = c.shape[0]
    b = -(-nb // SUBLANES) * SUBLANES
    c = jnp.pad(c, ((0, b - nb), (0, 0)))
    nt = d3 // d
    out = pl.pallas_call(
        _ada_kernel,
        grid=(depth, nt),
        in_specs=[
            pl.BlockSpec((b, d), lambda i, j: (0, 0)),
            pl.BlockSpec((1, d, d), lambda i, j: (i, 0, j)),
            pl.BlockSpec((1, 1, d), lambda i, j: (i, 0, j)),
        ],
        out_specs=pl.BlockSpec((1, b, d), lambda i, j: (i, 0, j)),
        out_shape=jax.ShapeDtypeStruct((depth, b, d3), F32),
        name="ada",
    )(c, w_ada, b_ada.reshape(depth, 1, d3))
    return out[:, :nb]


def _pre_kernel(x_ref, g_ref, ada_ref, w_ref, u_ref, *, d):
    x = x_ref[0]
    ms = jnp.mean(x * x, axis=-1, keepdims=True)
    y = x * lax.rsqrt(ms + RMS_EPS) * g_ref[...]
    ada = ada_ref[0]
    h = (y * (1.0 + ada[:, d:2 * d]) + ada[:, 0:d]).astype(BF16)
    n_out = w_ref.shape[1]
    for n in range(n_out // d):
        u_ref[0, :, n * d:(n + 1) * d] = jnp.dot(
            h, w_ref[:, n * d:(n + 1) * d], preferred_element_type=F32).astype(BF16)


def _pre(x, g_pre, ada, w_in):
    b, s, d = x.shape
    n_out = w_in.shape[1]
    tm = ROW_TILE
    return pl.pallas_call(
        functools.partial(_pre_kernel, d=d),
        grid=(b, s // tm),
        in_specs=[
            pl.BlockSpec((1, tm, d), lambda i, j: (i, j, 0)),
            pl.BlockSpec((1, d), lambda i, j: (0, 0)),
            pl.BlockSpec((1, 1, 3 * d), lambda i, j: (i, 0, 0)),
            pl.BlockSpec((d, n_out), lambda i, j: (0, 0)),
        ],
        out_specs=pl.BlockSpec((1, tm, n_out), lambda i, j: (i, j, 0)),
        out_shape=jax.ShapeDtypeStruct((b, s, n_out), BF16),
        compiler_params=pltpu.CompilerParams(
            dimension_semantics=("arbitrary", "arbitrary"), vmem_limit_bytes=VMEM_LIMIT),
        name="pre",
    )(x, g_pre.reshape(1, d), ada.reshape(b, 1, 3 * d), w_in)


def _post_kernel(y_ref, z_ref, x_ref, g_ref, ada_ref, w_ref, o_ref, *, d):
    y = y_ref[0].astype(F32)
    z = z_ref[0].astype(F32)
    t = (y * (z * jax.nn.sigmoid(z))).astype(BF16)
    r = jnp.dot(t, w_ref[...], preferred_element_type=F32)
    ms = jnp.mean(r * r, axis=-1, keepdims=True)
    rn = r * lax.rsqrt(ms + RMS_EPS) * g_ref[...]
    gate = ada_ref[0][:, 2 * d:3 * d]
    o_ref[0] = x_ref[0] + gate * rn


def _post(y, u, x, g_post, ada, w_out):
    b, s, d = x.shape
    tm = ROW_TILE
    z_col = u.shape[2] // d - 1
    return pl.pallas_call(
        functools.partial(_post_kernel, d=d),
        grid=(b, s // tm),
        in_specs=[
            pl.BlockSpec((1, tm, d), lambda i, j: (i, j, 0)),
            pl.BlockSpec((1, tm, d), lambda i, j: (i, j, z_col)),
            pl.BlockSpec((1, tm, d), lambda i, j: (i, j, 0)),
            pl.BlockSpec((1, d), lambda i, j: (0, 0)),
            pl.BlockSpec((1, 1, 3 * d), lambda i, j: (i, 0, 0)),
            pl.BlockSpec((d, d), lambda i, j: (0, 0)),
        ],
        out_specs=pl.BlockSpec((1, tm, d), lambda i, j: (i, j, 0)),
        out_shape=jax.ShapeDtypeStruct((b, s, d), F32),
        compiler_params=pltpu.CompilerParams(
            dimension_semantics=("arbitrary", "arbitrary"), vmem_limit_bytes=VMEM_LIMIT),
        name="post",
    )(y, u, x, g_post.reshape(1, d), ada.reshape(b, 1, 3 * d), w_out)


def _sb_kernel(q_ref, k_ref, v_ref, o_ref, *, t):
    qi = pl.program_id(2)
    q = q_ref[0].astype(F32) * (SB_HEAD_DIM ** -0.5)
    lane = lax.broadcasted_iota(jnp.int32, (t, LANES), 1)
    row = lax.broadcasted_iota(jnp.int32, (t, t), 0)
    col = lax.broadcasted_iota(jnp.int32, (t, t), 1)
    strict = col < row
    tri = jnp.where(row > col, 1.0, 0.0).astype(BF16)

    def tile(qm, j, carry, acc, diag):
        start = pl.multiple_of(j * t, t)
        k2 = k_ref[0, pl.ds(start, t), :]
        v2 = v_ref[0, pl.ds(start, t), :]
        z = lax.dot_general(qm, k2, _NT, preferred_element_type=F32)
        sp = jnp.log(1.0 + jnp.exp(-jnp.abs(z)))
        log_keep = -(jnp.maximum(z, 0.0) + sp)
        log_beta = jnp.minimum(z, 0.0) - sp
        if diag:
            log_keep = jnp.where(strict, log_keep, 0.0)
        lk = log_keep.astype(BF16)
        between = jnp.dot(lk, tri, preferred_element_type=F32)
        w = jnp.exp(log_beta + between + carry)
        if diag:
            w = jnp.where(strict, w, 0.0)
        acc = acc + jnp.dot(w.astype(BF16), v2, preferred_element_type=F32)
        carry = carry + between[:, 0:1] + log_keep[:, 0:1]
        return carry, acc

    def head(lo):
        keep = (lane < SB_HEAD_DIM) if lo else (lane >= SB_HEAD_DIM)
        qm = jnp.where(keep, q, 0.0).astype(BF16)
        carry = jnp.zeros((t, 1), F32)
        acc = jnp.zeros((t, LANES), F32)
        carry, acc = tile(qm, qi, carry, acc, True)

        def body(n, c):
            return tile(qm, qi - 1 - n, c[0], c[1], False)

        carry, acc = lax.fori_loop(0, qi, body, (carry, acc))
        return acc

    acc_lo = head(True)
    acc_hi = head(False)
    o_ref[0] = jnp.where(lane < SB_HEAD_DIM, acc_lo, acc_hi).astype(o_ref.dtype)


def _sb_attention(u, d):
    b, s, _ = u.shape
    t = ATT_TILE
    npair = d // LANES
    return pl.pallas_call(
        functools.partial(_sb_kernel, t=t),
        grid=(b, npair, s // t),
        in_specs=[
            pl.BlockSpec((1, t, LANES), lambda i, h, j: (i, j, h)),
            pl.BlockSpec((1, s, LANES), lambda i, h, j: (i, 0, npair + h)),
            pl.BlockSpec((1, s, LANES), lambda i, h, j: (i, 0, 2 * npair + h)),
        ],
        out_specs=pl.BlockSpec((1, t, LANES), lambda i, h, j: (i, j, h)),
        out_shape=jax.ShapeDtypeStruct((b, s, d), BF16),
        compiler_params=pltpu.CompilerParams(
            dimension_semantics=("arbitrary", "arbitrary", "arbitrary"),
            vmem_limit_bytes=VMEM_LIMIT),
        name="sb_attn",
    )(u, u, u)


def _diff_kernel(q_ref, k_ref, v_ref, lam_ref, g_ref, o_ref, *, t, n_heads, lam_init):
    h = pl.program_id(1)
    qi = pl.program_id(2)
    head_no = (lax.broadcasted_iota(jnp.int32, (1, 1), 0) + (h + 1)).astype(F32)
    slope = jnp.exp2(-head_no * (ALIBI_MAX_EXP / n_heads))
    q = q_ref[0].astype(F32) * (DIFF_HEAD_DIM ** -0.5)
    lane = lax.broadcasted_iota(jnp.int32, (t, LANES), 1)
    q1 = jnp.where(lane < DIFF_HEAD_DIM, q, 0.0).astype(BF16)
    q2 = jnp.where(lane < DIFF_HEAD_DIM, 0.0, q).astype(BF16)
    row = lax.broadcasted_iota(jnp.int32, (t, t), 0)
    col = lax.broadcasted_iota(jnp.int32, (t, t), 1)
    rel = (row - col).astype(F32)
    near = -slope * rel

    def softmax_step(s, m, l, acc, v2):
        m_new = jnp.maximum(m, jnp.max(s, axis=-1, keepdims=True))
        alpha = jnp.exp(m - m_new)
        p = jnp.exp(s - m_new)
        l = alpha * l + jnp.sum(p, axis=-1, keepdims=True)
        acc = alpha * acc + jnp.dot(p.astype(BF16), v2, preferred_element_type=F32)
        return m_new, l, acc

    def tile(j, state, diag):
        start = pl.multiple_of(j * t, t)
        k2 = k_ref[0, pl.ds(start, t), :]
        v2 = v_ref[0, pl.ds(start, t), :]
        s1 = lax.dot_general(q1, k2, _NT, preferred_element_type=F32)
        s2 = lax.dot_general(q2, k2, _NT, preferred_element_type=F32)
        if diag:
            bias = jnp.where(col // CHUNK <= row // CHUNK, -slope * jnp.abs(rel), NEG_INF)
        else:
            offset = (lax.broadcasted_iota(jnp.int32, (1, 1), 0) + (qi - j) * t).astype(F32)
            bias = near - slope * offset
        m1, l1, a1, m2, l2, a2 = state
        m1, l1, a1 = softmax_step(s1 + bias, m1, l1, a1, v2)
        m2, l2, a2 = softmax_step(s2 + bias, m2, l2, a2, v2)
        return m1, l1, a1, m2, l2, a2

    init = (jnp.full((t, 1), NEG_INF, F32), jnp.zeros((t, 1), F32), jnp.zeros((t, LANES), F32)) * 2
    state = tile(qi, init, True)
    state = lax.fori_loop(0, qi, lambda n, c: tile(qi - 1 - n, c, False), state)
    m1, l1, a1, m2, l2, a2 = state

    lam = lam_ref[...]
    lam_full = (jnp.exp(jnp.sum(lam[0:1] * lam[1:2], keepdims=True))
                - jnp.exp(jnp.sum(lam[2:3] * lam[3:4], keepdims=True)) + lam_init)
    out = a1 / l1 - lam_full * (a2 / l2)
    ms = jnp.mean(out * out, axis=-1, keepdims=True)
    out = out * lax.rsqrt(ms + RMS_EPS) * g_ref[...] * (1.0 - lam_init)
    o_ref[0] = out.astype(o_ref.dtype)


def _diff_attention(u, d, lam, g_sub, lam_init):
    b, s, _ = u.shape
    t = ATT_TILE
    n_heads = d // LANES
    return pl.pallas_call(
        functools.partial(_diff_kernel, t=t, n_heads=n_heads, lam_init=lam_init),
        grid=(b, n_heads, s // t),
        in_specs=[
            pl.BlockSpec((1, t, LANES), lambda i, h, j: (i, j, h)),
            pl.BlockSpec((1, s, LANES), lambda i, h, j: (i, 0, n_heads + h)),
            pl.BlockSpec((1, s, LANES), lambda i, h, j: (i, 0, 2 * n_heads + h)),
            pl.BlockSpec(lam.shape, lambda i, h, j: (0, 0)),
            pl.BlockSpec((1, LANES), lambda i, h, j: (0, 0)),
        ],
        out_specs=pl.BlockSpec((1, t, LANES), lambda i, h, j: (i, j, h)),
        out_shape=jax.ShapeDtypeStruct((b, s, d), BF16),
        compiler_params=pltpu.CompilerParams(
            dimension_semantics=("arbitrary", "arbitrary", "arbitrary"),
            vmem_limit_bytes=VMEM_LIMIT),
        name="diff_attn",
    )(u, u, u, lam, g_sub.reshape(1, LANES))


def kernel(x, c, w_ada, b_ada, g_pre, g_post, w_in, w_out, diff_lambda, diff_subln):
    depth = w_ada.shape[0]
    d = x.shape[-1]
    assert w_in.shape[2] == 4 * d and d % LANES == 0
    assert x.shape[1] % ROW_TILE == 0 and x.shape[1] % ATT_TILE == 0 and ATT_TILE % CHUNK == 0
    ada = _ada(c, w_ada, b_ada)
    w_in_b = w_in.astype(BF16)
    w_out_b = w_out.astype(BF16)
    for i in range(depth):
        u = _pre(x, g_pre[i], ada[i], w_in_b[i])
        if i % N_MIXERS == 0:
            y = _sb_attention(u, d)
        else:
            j = i // N_MIXERS
            y = _diff_attention(u, d, diff_lambda[j], diff_subln[j], _diff_lambda_init(i))
        x = _post(y, u, x, g_post[i], ada[i], w_out_b[i])
    return x
```

```python
import functools
import math

import jax
import jax.numpy as jnp
from jax import lax
from jax.experimental import pallas as pl
from jax.experimental.pallas import tpu as pltpu

F32 = jnp.float32
BF16 = jnp.bfloat16

LANES = 128
SUBLANES = 8
N_MIXERS = 2
CHUNK = 64
SB_HEAD_DIM = 64
DIFF_HEAD_DIM = 64
ALIBI_MAX_EXP = 8.0
RMS_EPS = 1e-6
NEG_INF = -1e30
LOG2E = 1.4426950408889634
SB_EXIT_BITS = 160.0

ROW_TILE = 512
ATT_TILE = 256
VMEM_LIMIT = 48 * 1024 * 1024

_NT = (((1,), (1,)), ((), ()))


def _diff_lambda_init(layer_idx):
    return 0.8 - 0.6 * math.exp(-0.3 * layer_idx)


def _ada_kernel(c_ref, w_ref, b_ref, o_ref):
    c = c_ref[...]
    cond = c * jax.nn.sigmoid(c)
    o_ref[0] = jnp.dot(cond, w_ref[0], preferred_element_type=F32,
                       precision=lax.Precision.HIGHEST) + b_ref[0]


def _ada(c, w_ada, b_ada):
    depth, d, d3 = w_ada.shape
    nb = c.shape[0]
    b = -(-nb // SUBLANES) * SUBLANES
    c = jnp.pad(c, ((0, b - nb), (0, 0)))
    nt = d3 // d
    out = pl.pallas_call(
        _ada_kernel,
        grid=(depth, nt),
        in_specs=[
            pl.BlockSpec((b, d), lambda i, j: (0, 0)),
            pl.BlockSpec((1, d, d), lambda i, j: (i, 0, j)),
            pl.BlockSpec((1, 1, d), lambda i, j: (i, 0, j)),
        ],
        out_specs=pl.BlockSpec((1, b, d), lambda i, j: (i, 0, j)),
        out_shape=jax.ShapeDtypeStruct((depth, b, d3), F32),
        name="ada",
    )(c, w_ada, b_ada.reshape(depth, 1, d3))
    return out[:, :nb]


def _pre_kernel(x_ref, g_ref, ada_ref, w_ref, u_ref, *, d):
    x = x_ref[0]
    ms = jnp.mean(x * x, axis=-1, keepdims=True)
    y = x * lax.rsqrt(ms + RMS_EPS) * g_ref[...]
    ada = ada_ref[0]
    h = (y * (1.0 + ada[:, d:2 * d]) + ada[:, 0:d]).astype(BF16)
    n_out = w_ref.shape[1]
    for n in range(n_out // d):
        u_ref[0, :, n * d:(n + 1) * d] = jnp.dot(
            h, w_ref[:, n * d:(n + 1) * d], preferred_element_type=F32).astype(BF16)


def _pre(x, g_pre, ada, w_in):
    b, s, d = x.shape
    n_out = w_in.shape[1]
    tm = ROW_TILE
    return pl.pallas_call(
        functools.partial(_pre_kernel, d=d),
        grid=(b, s // tm),
        in_specs=[
            pl.BlockSpec((1, tm, d), lambda i, j: (i, j, 0)),
            pl.BlockSpec((1, d), lambda i, j: (0, 0)),
            pl.BlockSpec((1, 1, 3 * d), lambda i, j: (i, 0, 0)),
            pl.BlockSpec((d, n_out), lambda i, j: (0, 0)),
        ],
        out_specs=pl.BlockSpec((1, tm, n_out), lambda i, j: (i, j, 0)),
        out_shape=jax.ShapeDtypeStruct((b, s, n_out), BF16),
        compiler_params=pltpu.CompilerParams(
            dimension_semantics=("arbitrary", "arbitrary"), vmem_limit_bytes=VMEM_LIMIT),
        name="pre",
    )(x, g_pre.reshape(1, d), ada.reshape(b, 1, 3 * d), w_in)


def _post_kernel(y_ref, z_ref, x_ref, g_ref, ada_ref, w_ref, o_ref, *, d):
    y = y_ref[0].astype(F32)
    z = z_ref[0].astype(F32)
    t = (y * (z * jax.nn.sigmoid(z))).astype(BF16)
    r = jnp.dot(t, w_ref[...], preferred_element_type=F32)
    ms = jnp.mean(r * r, axis=-1, keepdims=True)
    rn = r * lax.rsqrt(ms + RMS_EPS) * g_ref[...]
    gate = ada_ref[0][:, 2 * d:3 * d]
    o_ref[0] = x_ref[0] + gate * rn


def _post(y, u, x, g_post, ada, w_out):
    b, s, d = x.shape
    tm = ROW_TILE
    z_col = u.shape[2] // d - 1
    return pl.pallas_call(
        functools.partial(_post_kernel, d=d),
        grid=(b, s // tm),
        in_specs=[
            pl.BlockSpec((1, tm, d), lambda i, j: (i, j, 0)),
            pl.BlockSpec((1, tm, d), lambda i, j: (i, j, z_col)),
            pl.BlockSpec((1, tm, d), lambda i, j: (i, j, 0)),
            pl.BlockSpec((1, d), lambda i, j: (0, 0)),
            pl.BlockSpec((1, 1, 3 * d), lambda i, j: (i, 0, 0)),
            pl.BlockSpec((d, d), lambda i, j: (0, 0)),
        ],
        out_specs=pl.BlockSpec((1, tm, d), lambda i, j: (i, j, 0)),
        out_shape=jax.ShapeDtypeStruct((b, s, d), F32),
        compiler_params=pltpu.CompilerParams(
            dimension_semantics=("arbitrary", "arbitrary"), vmem_limit_bytes=VMEM_LIMIT),
        name="post",
    )(y, u, x, g_post.reshape(1, d), ada.reshape(b, 1, 3 * d), w_out)


def _sb_kernel(q_ref, k_ref, v_ref, o_ref, *, t):
    qi = pl.program_id(2)
    q = q_ref[0].astype(F32) * (-(SB_HEAD_DIM ** -0.5) * LOG2E)
    lane = lax.broadcasted_iota(jnp.int32, (t, LANES), 1)
    q_lo = jnp.where(lane < SB_HEAD_DIM, q, 0.0).astype(BF16)
    q_hi = jnp.where(lane < SB_HEAD_DIM, 0.0, q).astype(BF16)
    row = lax.broadcasted_iota(jnp.int32, (t, t), 0)
    col = lax.broadcasted_iota(jnp.int32, (t, t), 1)
    strict = col < row
    tri = jnp.where(row > col, 1.0, 0.0).astype(BF16)

    def chain(qm, k2, v2, carry, acc, diag):
        n = lax.dot_general(qm, k2, _NT, preferred_element_type=F32)
        sp = jnp.log2(1.0 + jnp.exp2(-jnp.abs(n)))
        log_keep = jnp.minimum(n, 0.0) - sp
        log_beta = log_keep - n
        if diag:
            log_keep = jnp.where(strict, log_keep, 0.0)
        between = jnp.dot(log_keep.astype(BF16), tri, preferred_element_type=F32)
        w = jnp.exp2(log_beta + between + carry)
        if diag:
            w = jnp.where(strict, w, 0.0)
        acc = acc + jnp.dot(w.astype(BF16), v2, preferred_element_type=F32)
        carry = carry + between[:, 0:1] + log_keep[:, 0:1]
        return carry, acc

    def tile(j, state, diag):
        start = pl.multiple_of(j * t, t)
        k2 = k_ref[0, pl.ds(start, t), :]
        v2 = v_ref[0, pl.ds(start, t), :]
        c_lo, a_lo, c_hi, a_hi = state
        c_lo, a_lo = chain(q_lo, k2, v2, c_lo, a_lo, diag)
        c_hi, a_hi = chain(q_hi, k2, v2, c_hi, a_hi, diag)
        return c_lo, a_lo, c_hi, a_hi

    def live(state):
        return jnp.max(jnp.maximum(state[0], state[2])) > -SB_EXIT_BITS

    zero_c = jnp.zeros((t, 1), F32)
    zero_a = jnp.zeros((t, LANES), F32)
    state = tile(qi, (zero_c, zero_a, zero_c, zero_a), True)

    def cond(c):
        return jnp.logical_and(c[0] >= 0, c[1])

    def body(c):
        state = tile(c[0], c[2], False)
        return c[0] - 1, live(state), state

    _, _, state = lax.while_loop(cond, body, (qi - 1, live(state), state))
    o_ref[0] = jnp.where(lane < SB_HEAD_DIM, state[1], state[3]).astype(o_ref.dtype)


def _sb_attention(u, d):
    b, s, _ = u.shape
    t = ATT_TILE
    npair = d // LANES
    return pl.pallas_call(
        functools.partial(_sb_kernel, t=t),
        grid=(b, npair, s // t),
        in_specs=[
            pl.BlockSpec((1, t, LANES), lambda i, h, j: (i, j, h)),
            pl.BlockSpec((1, s, LANES), lambda i, h, j: (i, 0, npair + h)),
            pl.BlockSpec((1, s, LANES), lambda i, h, j: (i, 0, 2 * npair + h)),
        ],
        out_specs=pl.BlockSpec((1, t, LANES), lambda i, h, j: (i, j, h)),
        out_shape=jax.ShapeDtypeStruct((b, s, d), BF16),
        compiler_params=pltpu.CompilerParams(
            dimension_semantics=("arbitrary", "arbitrary", "arbitrary"),
            vmem_limit_bytes=VMEM_LIMIT),
        name="sb_attn",
    )(u, u, u)


def _diff_kernel(q_ref, k_ref, v_ref, lam_ref, g_ref, o_ref, *, t, n_heads, lam_init):
    h = pl.program_id(1)
    qi = pl.program_id(2)
    head_no = (lax.broadcasted_iota(jnp.int32, (1, 1), 0) + (h + 1)).astype(F32)
    slope = jnp.exp2(-head_no * (ALIBI_MAX_EXP / n_heads))
    q = q_ref[0].astype(F32) * (DIFF_HEAD_DIM ** -0.5)
    lane = lax.broadcasted_iota(jnp.int32, (t, LANES), 1)
    q1 = jnp.where(lane < DIFF_HEAD_DIM, q, 0.0).astype(BF16)
    q2 = jnp.where(lane < DIFF_HEAD_DIM, 0.0, q).astype(BF16)
    row = lax.broadcasted_iota(jnp.int32, (t, t), 0)
    col = lax.broadcasted_iota(jnp.int32, (t, t), 1)
    rel = (row - col).astype(F32)
    near = -slope * rel

    def softmax_step(s, m, l, acc, v2):
        m_new = jnp.maximum(m, jnp.max(s, axis=-1, keepdims=True))
        alpha = jnp.exp(m - m_new)
        p = jnp.exp(s - m_new)
        l = alpha * l + jnp.sum(p, axis=-1, keepdims=True)
        acc = alpha * acc + jnp.dot(p.astype(BF16), v2, preferred_element_type=F32)
        return m_new, l, acc

    def tile(j, state, diag):
        start = pl.multiple_of(j * t, t)
        k2 = k_ref[0, pl.ds(start, t), :]
        v2 = v_ref[0, pl.ds(start, t), :]
        s1 = lax.dot_general(q1, k2, _NT, preferred_element_type=F32)
        s2 = lax.dot_general(q2, k2, _NT, preferred_element_type=F32)
        if diag:
            bias = jnp.where(col // CHUNK <= row // CHUNK, -slope * jnp.abs(rel), NEG_INF)
        else:
            offset = (lax.broadcasted_iota(jnp.int32, (1, 1), 0) + (qi - j) * t).astype(F32)
            bias = near - slope * offset
        m1, l1, a1, m2, l2, a2 = state
        m1, l1, a1 = softmax_step(s1 + bias, m1, l1, a1, v2)
        m2, l2, a2 = softmax_step(s2 + bias, m2, l2, a2, v2)
        return m1, l1, a1, m2, l2, a2

    init = (jnp.full((t, 1), NEG_INF, F32), jnp.zeros((t, 1), F32), jnp.zeros((t, LANES), F32)) * 2
    state = tile(qi, init, True)
    state = lax.fori_loop(0, qi, lambda n, c: tile(qi - 1 - n, c, False), state)
    m1, l1, a1, m2, l2, a2 = state

    lam = lam_ref[...]
    lam_full = (jnp.exp(jnp.sum(lam[0:1] * lam[1:2], keepdims=True))
                - jnp.exp(jnp.sum(lam[2:3] * lam[3:4], keepdims=True)) + lam_init)
    out = a1 / l1 - lam_full * (a2 / l2)
    ms = jnp.mean(out * out, axis=-1, keepdims=True)
    out = out * lax.rsqrt(ms + RMS_EPS) * g_ref[...] * (1.0 - lam_init)
    o_ref[0] = out.astype(o_ref.dtype)


def _diff_attention(u, d, lam, g_sub, lam_init):
    b, s, _ = u.shape
    t = ATT_TILE
    n_heads = d // LANES
    return pl.pallas_call(
        functools.partial(_diff_kernel, t=t, n_heads=n_heads, lam_init=lam_init),
        grid=(b, n_heads, s // t),
        in_specs=[
            pl.BlockSpec((1, t, LANES), lambda i, h, j: (i, j, h)),
            pl.BlockSpec((1, s, LANES), lambda i, h, j: (i, 0, n_heads + h)),
            pl.BlockSpec((1, s, LANES), lambda i, h, j: (i, 0, 2 * n_heads + h)),
            pl.BlockSpec(lam.shape, lambda i, h, j: (0, 0)),
            pl.BlockSpec((1, LANES), lambda i, h, j: (0, 0)),
        ],
        out_specs=pl.BlockSpec((1, t, LANES), lambda i, h, j: (i, j, h)),
        out_shape=jax.ShapeDtypeStruct((b, s, d), BF16),
        compiler_params=pltpu.CompilerParams(
            dimension_semantics=("arbitrary", "arbitrary", "arbitrary"),
            vmem_limit_bytes=VMEM_LIMIT),
        name="diff_attn",
    )(u, u, u, lam, g_sub.reshape(1, LANES))


def kernel(x, c, w_ada, b_ada, g_pre, g_post, w_in, w_out, diff_lambda, diff_subln):
    depth = w_ada.shape[0]
    d = x.shape[-1]
    assert w_in.shape[2] == 4 * d and d % LANES == 0
    assert x.shape[1] % ROW_TILE == 0 and x.shape[1] % ATT_TILE == 0 and ATT_TILE % CHUNK == 0
    ada = _ada(c, w_ada, b_ada)
    w_in_b = w_in.astype(BF16)
    w_out_b = w_out.astype(BF16)
    for i in range(depth):
        u = _pre(x, g_pre[i], ada[i], w_in_b[i])
        if i % N_MIXERS == 0:
            y = _sb_attention(u, d)
        else:
            j = i // N_MIXERS
            y = _diff_attention(u, d, diff_lambda[j], diff_subln[j], _diff_lambda_init(i))
        x = _post(y, u, x, g_post[i], ada[i], w_out_b[i])
    return x
```

```python
import functools
import math

import jax
import jax.numpy as jnp
from jax import lax
from jax.experimental import pallas as pl
from jax.experimental.pallas import tpu as pltpu

F32 = jnp.float32
BF16 = jnp.bfloat16

LANES = 128
SUBLANES = 8
N_MIXERS = 2
CHUNK = 64
SB_HEAD_DIM = 64
DIFF_HEAD_DIM = 64
ALIBI_MAX_EXP = 8.0
RMS_EPS = 1e-6
NEG_INF = -1e30
LOG2E = 1.4426950408889634
SB_EXIT_BITS = 160.0
ROWS_ONES = 16

ROW_TILE = 512
ATT_TILE = 256
VMEM_LIMIT = 48 * 1024 * 1024

_NT = (((1,), (1,)), ((), ()))


def _diff_lambda_init(layer_idx):
    return 0.8 - 0.6 * math.exp(-0.3 * layer_idx)


def _ada_kernel(c_ref, w_ref, b_ref, o_ref):
    c = c_ref[...]
    cond = c * jax.nn.sigmoid(c)
    o_ref[0] = jnp.dot(cond, w_ref[0], preferred_element_type=F32,
                       precision=lax.Precision.HIGHEST) + b_ref[0]


def _ada(c, w_ada, b_ada):
    depth, d, d3 = w_ada.shape
    nb = c.shape[0]
    b = -(-nb // SUBLANES) * SUBLANES
    c = jnp.pad(c, ((0, b - nb), (0, 0)))
    nt = d3 // d
    out = pl.pallas_call(
        _ada_kernel,
        grid=(depth, nt),
        in_specs=[
            pl.BlockSpec((b, d), lambda i, j: (0, 0)),
            pl.BlockSpec((1, d, d), lambda i, j: (i, 0, j)),
            pl.BlockSpec((1, 1, d), lambda i, j: (i, 0, j)),
        ],
        out_specs=pl.BlockSpec((1, b, d), lambda i, j: (i, 0, j)),
        out_shape=jax.ShapeDtypeStruct((depth, b, d3), F32),
        name="ada",
    )(c, w_ada, b_ada.reshape(depth, 1, d3))
    return out[:, :nb]


def _pre_kernel(x_ref, g_ref, ada_ref, w_ref, u_ref, *, d):
    x = x_ref[0]
    ms = jnp.mean(x * x, axis=-1, keepdims=True)
    y = x * lax.rsqrt(ms + RMS_EPS) * g_ref[...]
    ada = ada_ref[0]
    h = (y * (1.0 + ada[:, d:2 * d]) + ada[:, 0:d]).astype(BF16)
    n_out = w_ref.shape[1]
    for n in range(n_out // d):
        u_ref[0, :, n * d:(n + 1) * d] = jnp.dot(
            h, w_ref[:, n * d:(n + 1) * d], preferred_element_type=F32).astype(BF16)


def _pre(x, g_pre, ada, w_in):
    b, s, d = x.shape
    n_out = w_in.shape[1]
    tm = ROW_TILE
    return pl.pallas_call(
        functools.partial(_pre_kernel, d=d),
        grid=(b, s // tm),
        in_specs=[
            pl.BlockSpec((1, tm, d), lambda i, j: (i, j, 0)),
            pl.BlockSpec((1, d), lambda i, j: (0, 0)),
            pl.BlockSpec((1, 1, 3 * d), lambda i, j: (i, 0, 0)),
            pl.BlockSpec((d, n_out), lambda i, j: (0, 0)),
        ],
        out_specs=pl.BlockSpec((1, tm, n_out), lambda i, j: (i, j, 0)),
        out_shape=jax.ShapeDtypeStruct((b, s, n_out), BF16),
        compiler_params=pltpu.CompilerParams(
            dimension_semantics=("arbitrary", "arbitrary"), vmem_limit_bytes=VMEM_LIMIT),
        name="pre",
    )(x, g_pre.reshape(1, d), ada.reshape(b, 1, 3 * d), w_in)


def _post_kernel(y_ref, z_ref, x_ref, g_ref, ada_ref, w_ref, o_ref, *, d):
    y = y_ref[0].astype(F32)
    z = z_ref[0].astype(F32)
    t = (y * (z * jax.nn.sigmoid(z))).astype(BF16)
    r = jnp.dot(t, w_ref[...], preferred_element_type=F32)
    ms = jnp.mean(r * r, axis=-1, keepdims=True)
    rn = r * lax.rsqrt(ms + RMS_EPS) * g_ref[...]
    gate = ada_ref[0][:, 2 * d:3 * d]
    o_ref[0] = x_ref[0] + gate * rn


def _post(y, u, x, g_post, ada, w_out):
    b, s, d = x.shape
    tm = ROW_TILE
    z_col = u.shape[2] // d - 1
    return pl.pallas_call(
        functools.partial(_post_kernel, d=d),
        grid=(b, s // tm),
        in_specs=[
            pl.BlockSpec((1, tm, d), lambda i, j: (i, j, 0)),
            pl.BlockSpec((1, tm, d), lambda i, j: (i, j, z_col)),
            pl.BlockSpec((1, tm, d), lambda i, j: (i, j, 0)),
            pl.BlockSpec((1, d), lambda i, j: (0, 0)),
            pl.BlockSpec((1, 1, 3 * d), lambda i, j: (i, 0, 0)),
            pl.BlockSpec((d, d), lambda i, j: (0, 0)),
        ],
        out_specs=pl.BlockSpec((1, tm, d), lambda i, j: (i, j, 0)),
        out_shape=jax.ShapeDtypeStruct((b, s, d), F32),
        compiler_params=pltpu.CompilerParams(
            dimension_semantics=("arbitrary", "arbitrary"), vmem_limit_bytes=VMEM_LIMIT),
        name="post",
    )(y, u, x, g_post.reshape(1, d), ada.reshape(b, 1, 3 * d), w_out)


def _sb_kernel(q_ref, k_ref, v_ref, o_ref, *, t):
    qi = pl.program_id(2)
    q = q_ref[0].astype(F32) * (-(SB_HEAD_DIM ** -0.5) * LOG2E)
    lane = lax.broadcasted_iota(jnp.int32, (t, LANES), 1)
    q_lo = jnp.where(lane < SB_HEAD_DIM, q, 0.0).astype(BF16)
    q_hi = jnp.where(lane < SB_HEAD_DIM, 0.0, q).astype(BF16)
    row = lax.broadcasted_iota(jnp.int32, (t, t), 0)
    col = lax.broadcasted_iota(jnp.int32, (t, t), 1)
    strict = col < row
    tri = jnp.where(row > col, 1.0, 0.0).astype(BF16)

    def chain(qm, k2, v2, carry, acc, diag):
        n = lax.dot_general(qm, k2, _NT, preferred_element_type=F32)
        sp = jnp.log2(1.0 + jnp.exp2(-jnp.abs(n)))
        log_keep = jnp.minimum(n, 0.0) - sp
        log_beta = log_keep - n
        if diag:
            log_keep = jnp.where(strict, log_keep, 0.0)
        between = jnp.dot(log_keep.astype(BF16), tri, preferred_element_type=F32)
        w = jnp.exp2(log_beta + between + carry)
        if diag:
            w = jnp.where(strict, w, 0.0)
        acc = acc + jnp.dot(w.astype(BF16), v2, preferred_element_type=F32)
        carry = carry + between[:, 0:1] + log_keep[:, 0:1]
        return carry, acc

    def tile(j, state, diag):
        start = pl.multiple_of(j * t, t)
        k2 = k_ref[0, pl.ds(start, t), :]
        v2 = v_ref[0, pl.ds(start, t), :]
        c_lo, a_lo, c_hi, a_hi = state
        c_lo, a_lo = chain(q_lo, k2, v2, c_lo, a_lo, diag)
        c_hi, a_hi = chain(q_hi, k2, v2, c_hi, a_hi, diag)
        return c_lo, a_lo, c_hi, a_hi

    def live(state):
        return jnp.max(jnp.maximum(state[0], state[2])) > -SB_EXIT_BITS

    zero_c = jnp.zeros((t, 1), F32)
    zero_a = jnp.zeros((t, LANES), F32)
    state = tile(qi, (zero_c, zero_a, zero_c, zero_a), True)

    def cond(c):
        return jnp.logical_and(c[0] >= 0, c[1])

    def body(c):
        state = tile(c[0], c[2], False)
        return c[0] - 1, live(state), state

    _, _, state = lax.while_loop(cond, body, (qi - 1, live(state), state))
    o_ref[0] = jnp.where(lane < SB_HEAD_DIM, state[1], state[3]).astype(o_ref.dtype)


def _sb_attention(u, d):
    b, s, _ = u.shape
    t = ATT_TILE
    npair = d // LANES
    return pl.pallas_call(
        functools.partial(_sb_kernel, t=t),
        grid=(b, npair, s // t),
        in_specs=[
            pl.BlockSpec((1, t, LANES), lambda i, h, j: (i, j, h)),
            pl.BlockSpec((1, s, LANES), lambda i, h, j: (i, 0, npair + h)),
            pl.BlockSpec((1, s, LANES), lambda i, h, j: (i, 0, 2 * npair + h)),
        ],
        out_specs=pl.BlockSpec((1, t, LANES), lambda i, h, j: (i, j, h)),
        out_shape=jax.ShapeDtypeStruct((b, s, d), BF16),
        compiler_params=pltpu.CompilerParams(
            dimension_semantics=("arbitrary", "arbitrary", "arbitrary"),
            vmem_limit_bytes=VMEM_LIMIT),
        name="sb_attn",
    )(u, u, u)


def _diff_kernel(q_ref, k_ref, v_ref, lam_ref, g_ref, o_ref, vt_ref, *, t, n_heads, lam_init):
    h = pl.program_id(1)
    qi = pl.program_id(2)
    n_tiles = k_ref.shape[1] // t
    head_no = (lax.broadcasted_iota(jnp.int32, (1, 1), 0) + (h + 1)).astype(F32)
    slope = jnp.exp2(-head_no * (ALIBI_MAX_EXP / n_heads))

    @pl.when(qi == 0)
    def _():
        eye = jnp.where(lax.broadcasted_iota(jnp.int32, (LANES, LANES), 0)
                        == lax.broadcasted_iota(jnp.int32, (LANES, LANES), 1), 1.0, 0.0).astype(BF16)

        def fill(n, carry):
            vt = lax.dot_general(eye, v_ref[0, pl.ds(pl.multiple_of(n * t, t), t), :], _NT,
                                 preferred_element_type=F32)
            vt_ref[n, 0:LANES, :] = vt.astype(BF16)
            vt_ref[n, LANES:LANES + ROWS_ONES, :] = jnp.ones((ROWS_ONES, t), BF16)
            return carry

        lax.fori_loop(0, n_tiles, fill, 0)

    lane = lax.broadcasted_iota(jnp.int32, (t, LANES), 1)
    sub = lax.broadcasted_iota(jnp.int32, (t, LANES), 0)
    q = q_ref[0].astype(F32) * (DIFF_HEAD_DIM ** -0.5)
    one_col = jnp.where(lane == 0, 1.0, 0.0).astype(BF16)
    q1 = jnp.concatenate([jnp.where(lane < DIFF_HEAD_DIM, q, 0.0).astype(BF16), one_col], axis=1)
    q2 = jnp.concatenate([jnp.where(lane < DIFF_HEAD_DIM, 0.0, q).astype(BF16), one_col], axis=1)
    k_aug = jnp.where(lane == 0, slope * sub.astype(F32), 0.0).astype(BF16)
    q_pos = lax.broadcasted_iota(jnp.int32, (1, t), 1)

    def key_tile(j):
        k2 = k_ref[0, pl.ds(pl.multiple_of(j * t, t), t), :]
        return jnp.concatenate([k2, k_aug], axis=1)

    def query_bias(j):
        return -slope * (q_pos + (qi - j) * t).astype(F32)

    def update(scores, biases, vt, m, acc):
        m_new = m
        for s, c in zip(scores, biases):
            m_new = jnp.maximum(m_new, jnp.max(s, axis=0, keepdims=True) + c)
        alpha = jnp.exp(m - m_new)
        p = [jnp.exp(s - (m_new - c)).astype(BF16) for s, c in zip(scores, biases)]
        p = p[0] if len(p) == 1 else jnp.concatenate(p, axis=0)
        acc = alpha * acc + jnp.dot(vt, p, preferred_element_type=F32)
        return m_new, acc

    kpos = lax.broadcasted_iota(jnp.int32, (t, t), 0)
    qpos = lax.broadcasted_iota(jnp.int32, (t, t), 1)
    diag_bias = jnp.where(kpos // CHUNK <= qpos // CHUNK,
                          -slope * (jnp.abs(qpos - kpos) + kpos).astype(F32), NEG_INF)
    kd = key_tile(qi)
    vd = vt_ref[qi]
    zero_c = jnp.zeros((1, t), F32)
    m0 = jnp.full((1, t), NEG_INF, F32)
    a0 = jnp.zeros((LANES + ROWS_ONES, t), F32)
    s1 = lax.dot_general(kd, q1, _NT, preferred_element_type=F32) + diag_bias
    s2 = lax.dot_general(kd, q2, _NT, preferred_element_type=F32) + diag_bias
    m1, a1 = update([s1], [zero_c], vd, m0, a0)
    m2, a2 = update([s2], [zero_c], vd, m0, a0)

    def step(n, state):
        m1, a1, m2, a2 = state
        j_hi = qi - 1 - 2 * n
        j_lo = jnp.maximum(j_hi - 1, 0)
        j_up = j_lo + 1
        c_lo = query_bias(j_lo)
        c_up = jnp.where(jnp.full((1, t), j_hi, jnp.int32) >= 1, query_bias(j_up), NEG_INF)
        k_lo = key_tile(j_lo)
        k_up = key_tile(j_up)
        vt = jnp.concatenate([vt_ref[j_lo], vt_ref[j_up]], axis=1)
        s = [lax.dot_general(k, qm, _NT, preferred_element_type=F32)
             for qm in (q1, q2) for k in (k_lo, k_up)]
        m1, a1 = update(s[0:2], [c_lo, c_up], vt, m1, a1)
        m2, a2 = update(s[2:4], [c_lo, c_up], vt, m2, a2)
        return m1, a1, m2, a2

    m1, a1, m2, a2 = lax.fori_loop(0, (qi + 1) // 2, step, (m1, a1, m2, a2))

    lam = lam_ref[...]
    lam_full = (jnp.exp(jnp.sum(lam[0:1] * lam[1:2], keepdims=True))
                - jnp.exp(jnp.sum(lam[2:3] * lam[3:4], keepdims=True)) + lam_init)
    out_t = (a1[0:LANES] / a1[LANES:LANES + 1]
             - lam_full * (a2[0:LANES] / a2[LANES:LANES + 1]))
    out = out_t.T
    ms = jnp.mean(out * out, axis=-1, keepdims=True)
    out = out * lax.rsqrt(ms + RMS_EPS) * g_ref[...] * (1.0 - lam_init)
    o_ref[0] = out.astype(o_ref.dtype)


def _diff_attention(u, d, lam, g_sub, lam_init):
    b, s, _ = u.shape
    t = ATT_TILE
    n_heads = d // LANES
    return pl.pallas_call(
        functools.partial(_diff_kernel, t=t, n_heads=n_heads, lam_init=lam_init),
        grid=(b, n_heads, s // t),
        in_specs=[
            pl.BlockSpec((1, t, LANES), lambda i, h, j: (i, j, h)),
            pl.BlockSpec((1, s, LANES), lambda i, h, j: (i, 0, n_heads + h)),
            pl.BlockSpec((1, s, LANES), lambda i, h, j: (i, 0, 2 * n_heads + h)),
            pl.BlockSpec(lam.shape, lambda i, h, j: (0, 0)),
            pl.BlockSpec((1, LANES), lambda i, h, j: (0, 0)),
        ],
        out_specs=pl.BlockSpec((1, t, LANES), lambda i, h, j: (i, j, h)),
        out_shape=jax.ShapeDtypeStruct((b, s, d), BF16),
        scratch_shapes=[pltpu.VMEM((s // t, LANES + ROWS_ONES, t), BF16)],
        compiler_params=pltpu.CompilerParams(
            dimension_semantics=("arbitrary", "arbitrary", "arbitrary"),
            vmem_limit_bytes=VMEM_LIMIT),
        name="diff_attn",
    )(u, u, u, lam, g_sub.reshape(1, LANES))


def kernel(x, c, w_ada, b_ada, g_pre, g_post, w_in, w_out, diff_lambda, diff_subln):
    depth = w_ada.shape[0]
    d = x.shape[-1]
    assert w_in.shape[2] == 4 * d and d % LANES == 0
    assert x.shape[1] % ROW_TILE == 0 and x.shape[1] % ATT_TILE == 0 and ATT_TILE % CHUNK == 0
    ada = _ada(c, w_ada, b_ada)
    w_in_b = w_in.astype(BF16)
    w_out_b = w_out.astype(BF16)
    for i in range(depth):
        u = _pre(x, g_pre[i], ada[i], w_in_b[i])
        if i % N_MIXERS == 0:
            y = _sb_attention(u, d)
        else:
            j = i // N_MIXERS
            y = _diff_attention(u, d, diff_lambda[j], diff_subln[j], _diff_lambda_init(i))
        x = _post(y, u, x, g_post[i], ada[i], w_out_b[i])
    return x
```

```python
import functools
import math

import jax
import jax.numpy as jnp
from jax import lax
from jax.experimental import pallas as pl
from jax.experimental.pallas import tpu as pltpu

F32 = jnp.float32
BF16 = jnp.bfloat16

LANES = 128
SUBLANES = 8
N_MIXERS = 2
CHUNK = 64
SB_HEAD_DIM = 64
DIFF_HEAD_DIM = 64
ALIBI_MAX_EXP = 8.0
RMS_EPS = 1e-6
NEG_INF = -1e30
LOG2E = 1.4426950408889634
SB_EXIT_BITS = 160.0
DIFF_GROUP = 4
EXP_ZERO_BELOW = 104.0
BOUND_SLACK = 1.05
ROWS_ONES = 16

ROW_TILE = 512
ATT_TILE = 256
VMEM_LIMIT = 48 * 1024 * 1024

_NT = (((1,), (1,)), ((), ()))


def _diff_lambda_init(layer_idx):
    return 0.8 - 0.6 * math.exp(-0.3 * layer_idx)


def _ada_kernel(c_ref, w_ref, b_ref, o_ref):
    c = c_ref[...]
    cond = c * jax.nn.sigmoid(c)
    o_ref[0] = jnp.dot(cond, w_ref[0], preferred_element_type=F32,
                       precision=lax.Precision.HIGHEST) + b_ref[0]


def _ada(c, w_ada, b_ada):
    depth, d, d3 = w_ada.shape
    nb = c.shape[0]
    b = -(-nb // SUBLANES) * SUBLANES
    c = jnp.pad(c, ((0, b - nb), (0, 0)))
    nt = d3 // d
    out = pl.pallas_call(
        _ada_kernel,
        grid=(depth, nt),
        in_specs=[
            pl.BlockSpec((b, d), lambda i, j: (0, 0)),
            pl.BlockSpec((1, d, d), lambda i, j: (i, 0, j)),
            pl.BlockSpec((1, 1, d), lambda i, j: (i, 0, j)),
        ],
        out_specs=pl.BlockSpec((1, b, d), lambda i, j: (i, 0, j)),
        out_shape=jax.ShapeDtypeStruct((depth, b, d3), F32),
        name="ada",
    )(c, w_ada, b_ada.reshape(depth, 1, d3))
    return out[:, :nb]


def _pre_kernel(x_ref, g_ref, ada_ref, w_ref, u_ref, *, d):
    x = x_ref[0]
    ms = jnp.mean(x * x, axis=-1, keepdims=True)
    y = x * lax.rsqrt(ms + RMS_EPS) * g_ref[...]
    ada = ada_ref[0]
    h = (y * (1.0 + ada[:, d:2 * d]) + ada[:, 0:d]).astype(BF16)
    n_out = w_ref.shape[1]
    for n in range(n_out // d):
        u_ref[0, :, n * d:(n + 1) * d] = jnp.dot(
            h, w_ref[:, n * d:(n + 1) * d], preferred_element_type=F32).astype(BF16)


def _pre(x, g_pre, ada, w_in):
    b, s, d = x.shape
    n_out = w_in.shape[1]
    tm = ROW_TILE
    return pl.pallas_call(
        functools.partial(_pre_kernel, d=d),
        grid=(b, s // tm),
        in_specs=[
            pl.BlockSpec((1, tm, d), lambda i, j: (i, j, 0)),
            pl.BlockSpec((1, d), lambda i, j: (0, 0)),
            pl.BlockSpec((1, 1, 3 * d), lambda i, j: (i, 0, 0)),
            pl.BlockSpec((d, n_out), lambda i, j: (0, 0)),
        ],
        out_specs=pl.BlockSpec((1, tm, n_out), lambda i, j: (i, j, 0)),
        out_shape=jax.ShapeDtypeStruct((b, s, n_out), BF16),
        compiler_params=pltpu.CompilerParams(
            dimension_semantics=("arbitrary", "arbitrary"), vmem_limit_bytes=VMEM_LIMIT),
        name="pre",
    )(x, g_pre.reshape(1, d), ada.reshape(b, 1, 3 * d), w_in)


def _post_kernel(y_ref, z_ref, x_ref, g_ref, ada_ref, w_ref, o_ref, *, d):
    y = y_ref[0].astype(F32)
    z = z_ref[0].astype(F32)
    t = (y * (z * jax.nn.sigmoid(z))).astype(BF16)
    r = jnp.dot(t, w_ref[...], preferred_element_type=F32)
    ms = jnp.mean(r * r, axis=-1, keepdims=True)
    rn = r * lax.rsqrt(ms + RMS_EPS) * g_ref[...]
    gate = ada_ref[0][:, 2 * d:3 * d]
    o_ref[0] = x_ref[0] + gate * rn


def _post(y, u, x, g_post, ada, w_out):
    b, s, d = x.shape
    tm = ROW_TILE
    z_col = u.shape[2] // d - 1
    return pl.pallas_call(
        functools.partial(_post_kernel, d=d),
        grid=(b, s // tm),
        in_specs=[
            pl.BlockSpec((1, tm, d), lambda i, j: (i, j, 0)),
            pl.BlockSpec((1, tm, d), lambda i, j: (i, j, z_col)),
            pl.BlockSpec((1, tm, d), lambda i, j: (i, j, 0)),
            pl.BlockSpec((1, d), lambda i, j: (0, 0)),
            pl.BlockSpec((1, 1, 3 * d), lambda i, j: (i, 0, 0)),
            pl.BlockSpec((d, d), lambda i, j: (0, 0)),
        ],
        out_specs=pl.BlockSpec((1, tm, d), lambda i, j: (i, j, 0)),
        out_shape=jax.ShapeDtypeStruct((b, s, d), F32),
        compiler_params=pltpu.CompilerParams(
            dimension_semantics=("arbitrary", "arbitrary"), vmem_limit_bytes=VMEM_LIMIT),
        name="post",
    )(y, u, x, g_post.reshape(1, d), ada.reshape(b, 1, 3 * d), w_out)


def _sb_kernel(q_ref, k_ref, v_ref, o_ref, acc_ref, *, t):
    qi = pl.program_id(2)
    q = q_ref[0].astype(F32) * (-(SB_HEAD_DIM ** -0.5) * LOG2E)
    lane = lax.broadcasted_iota(jnp.int32, (t, LANES), 1)
    q_lo = jnp.where(lane < SB_HEAD_DIM, q, 0.0).astype(BF16)
    q_hi = jnp.where(lane < SB_HEAD_DIM, 0.0, q).astype(BF16)
    row = lax.broadcasted_iota(jnp.int32, (t, t), 0)
    col = lax.broadcasted_iota(jnp.int32, (t, t), 1)
    strict = col < row
    tri = jnp.where(row > col, 1.0, 0.0).astype(BF16)

    def chain(qm, k2, v2, carry, diag):
        n = lax.dot_general(qm, k2, _NT, preferred_element_type=F32)
        sp = jnp.log2(1.0 + jnp.exp2(-jnp.abs(n)))
        log_keep = jnp.minimum(n, 0.0) - sp
        log_beta = log_keep - n
        if diag:
            log_keep = jnp.where(strict, log_keep, 0.0)
        between = jnp.dot(log_keep.astype(BF16), tri, preferred_element_type=F32)
        w = jnp.exp2(log_beta + between + carry)
        if diag:
            w = jnp.where(strict, w, 0.0)
        pv = jnp.dot(w.astype(BF16), v2, preferred_element_type=F32)
        return carry + between[:, 0:1] + log_keep[:, 0:1], pv

    def key_value(j):
        rows = pl.ds(pl.multiple_of(j * t, t), t)
        return k_ref[0, rows, :], v_ref[0, rows, :]

    kd, vd = key_value(qi)
    kp, vp = key_value(jnp.maximum(qi - 1, 0))
    has_prev = jnp.full((t, 1), qi, jnp.int32) >= 1
    carries = []
    for idx, qm in enumerate((q_lo, q_hi)):
        c, pv_d = chain(qm, kd, vd, jnp.zeros((t, 1), F32), True)
        c, pv_p = chain(qm, kp, vp, jnp.where(has_prev, c, NEG_INF), False)
        acc_ref[idx] = pv_d + pv_p
        carries.append(c)

    def live(c_lo, c_hi):
        return jnp.max(jnp.maximum(c_lo, c_hi)) > -SB_EXIT_BITS

    def cond(c):
        return jnp.logical_and(c[0] >= 0, c[1])

    def body(c):
        j, _, c_lo, c_hi = c
        k2, v2 = key_value(j)
        c_lo, pv_lo = chain(q_lo, k2, v2, c_lo, False)
        c_hi, pv_hi = chain(q_hi, k2, v2, c_hi, False)
        acc_ref[0] += pv_lo
        acc_ref[1] += pv_hi
        return j - 1, live(c_lo, c_hi), c_lo, c_hi

    lax.while_loop(cond, body, (qi - 2, live(*carries), *carries))
    o_ref[0] = jnp.where(lane < SB_HEAD_DIM, acc_ref[0], acc_ref[1]).astype(o_ref.dtype)


def _sb_attention(u, d):
    b, s, _ = u.shape
    t = ATT_TILE
    npair = d // LANES
    return pl.pallas_call(
        functools.partial(_sb_kernel, t=t),
        grid=(b, npair, s // t),
        in_specs=[
            pl.BlockSpec((1, t, LANES), lambda i, h, j: (i, j, h)),
            pl.BlockSpec((1, s, LANES), lambda i, h, j: (i, 0, npair + h)),
            pl.BlockSpec((1, s, LANES), lambda i, h, j: (i, 0, 2 * npair + h)),
        ],
        out_specs=pl.BlockSpec((1, t, LANES), lambda i, h, j: (i, j, h)),
        out_shape=jax.ShapeDtypeStruct((b, s, d), BF16),
        scratch_shapes=[pltpu.VMEM((2, t, LANES), F32)],
        compiler_params=pltpu.CompilerParams(
            dimension_semantics=("arbitrary", "arbitrary", "arbitrary"),
            vmem_limit_bytes=VMEM_LIMIT),
        name="sb_attn",
    )(u, u, u)


def _diff_kernel(q_ref, k_ref, v_ref, lam_ref, g_ref, o_ref, vt_ref, kmax_ref, acc_ref,
                 *, t, n_heads, lam_init):
    h = pl.program_id(1)
    qi = pl.program_id(2)
    n_tiles = k_ref.shape[1] // t
    head_no = (lax.broadcasted_iota(jnp.int32, (1, 1), 0) + (h + 1)).astype(F32)
    slope = jnp.exp2(-head_no * (ALIBI_MAX_EXP / n_heads))
    lane = lax.broadcasted_iota(jnp.int32, (t, LANES), 1)
    half = lax.broadcasted_iota(jnp.int32, (SUBLANES, LANES), 0)
    feat = lax.broadcasted_iota(jnp.int32, (SUBLANES, LANES), 1)
    pick = jnp.where((feat >= DIFF_HEAD_DIM) == (half == 1), 1.0, 0.0)
    pick = jnp.where(half < 2, pick, 0.0).astype(BF16)

    def sq_norms(x):
        xf = x.astype(F32)
        return lax.dot_general(pick, (xf * xf).astype(BF16), _NT, preferred_element_type=F32)

    @pl.when(qi == 0)
    def _():
        eye = jnp.where(lax.broadcasted_iota(jnp.int32, (LANES, LANES), 0)
                        == lax.broadcasted_iota(jnp.int32, (LANES, LANES), 1), 1.0, 0.0).astype(BF16)

        def fill(n, kmax):
            rows = pl.ds(pl.multiple_of(n * t, t), t)
            vt = lax.dot_general(eye, v_ref[0, rows, :], _NT, preferred_element_type=F32)
            vt_ref[n, 0:LANES, :] = vt.astype(BF16)
            vt_ref[n, LANES:LANES + ROWS_ONES, :] = jnp.ones((ROWS_ONES, t), BF16)
            kmax = jnp.maximum(kmax, jnp.max(sq_norms(k_ref[0, rows, :]), axis=1, keepdims=True))
            kmax_ref[n] = jnp.broadcast_to(kmax, (SUBLANES, LANES))
            return kmax

        lax.fori_loop(0, n_tiles, fill, jnp.zeros((SUBLANES, 1), F32))

    sub = lax.broadcasted_iota(jnp.int32, (t, LANES), 0)
    q_raw = q_ref[0]
    q = q_raw.astype(F32) * (DIFF_HEAD_DIM ** -0.5)
    one_col = jnp.where(lane == 0, 1.0, 0.0).astype(BF16)
    q1 = jnp.concatenate([jnp.where(lane < DIFF_HEAD_DIM, q, 0.0).astype(BF16), one_col], axis=1)
    q2 = jnp.concatenate([jnp.where(lane < DIFF_HEAD_DIM, 0.0, q).astype(BF16), one_col], axis=1)
    k_aug = jnp.where(lane == 0, slope * sub.astype(F32), 0.0).astype(BF16)
    q_pos = lax.broadcasted_iota(jnp.int32, (1, t), 1)
    q_sq = sq_norms(q_raw) * (BOUND_SLACK / DIFF_HEAD_DIM)

    def key_tile(j):
        k2 = k_ref[0, pl.ds(pl.multiple_of(j * t, t), t), :]
        return jnp.concatenate([k2, k_aug], axis=1)

    def query_bias(j):
        return -slope * (q_pos + (qi - j) * t).astype(F32)

    def update(idx, scores, biases, vt, m):
        m_new = m
        for s, c in zip(scores, biases):
            m_new = jnp.maximum(m_new, jnp.max(s, axis=0, keepdims=True) + c)
        alpha = jnp.exp(m - m_new)
        p = jnp.concatenate([jnp.exp(s - (m_new - c)).astype(BF16) for s, c in zip(scores, biases)], axis=0)
        acc_ref[idx] = alpha * acc_ref[idx] + jnp.dot(vt, p, preferred_element_type=F32)
        return m_new

    def group(j_top, m1, m2, diag_bias):
        js = [jnp.maximum(j_top - g, 0) for g in range(DIFF_GROUP)]
        cs = [jnp.where(jnp.full((1, t), j_top - g, jnp.int32) >= 0, query_bias(js[g]), NEG_INF)
              for g in range(DIFF_GROUP)]
        ks = [key_tile(j) for j in js]
        vt = jnp.concatenate([vt_ref[j] for j in js], axis=1)
        s1 = [lax.dot_general(k, q1, _NT, preferred_element_type=F32) for k in ks]
        s2 = [lax.dot_general(k, q2, _NT, preferred_element_type=F32) for k in ks]
        if diag_bias is not None:
            s1[0] = s1[0] + diag_bias
            s2[0] = s2[0] + diag_bias
            cs[0] = jnp.zeros((1, t), F32)
        return update(0, s1, cs, vt, m1), update(1, s2, cs, vt, m2)

    def more(j_top, m1, m2):
        k_sq = kmax_ref[jnp.maximum(j_top, 0)]
        reach = jnp.sqrt(q_sq * k_sq[:, 0:1])
        nearest = -slope * (q_pos + ((qi - j_top - 1) * t + 1)).astype(F32)
        gap = jnp.maximum(reach[0:1] - m1, reach[1:2] - m2) + nearest
        return jnp.max(gap) > -EXP_ZERO_BELOW

    kpos = lax.broadcasted_iota(jnp.int32, (t, t), 0)
    qpos = lax.broadcasted_iota(jnp.int32, (t, t), 1)
    diag_bias = jnp.where(kpos // CHUNK <= qpos // CHUNK,
                          -slope * (jnp.abs(qpos - kpos) + kpos).astype(F32), NEG_INF)
    acc_ref[...] = jnp.zeros(acc_ref.shape, F32)
    m0 = jnp.full((1, t), NEG_INF, F32)
    m1, m2 = group(qi, m0, m0, diag_bias)

    def cond(c):
        return jnp.logical_and(c[0] >= 0, c[1])

    def body(c):
        j_top, _, m1, m2 = c
        m1, m2 = group(j_top, m1, m2, None)
        j_next = j_top - DIFF_GROUP
        return j_next, more(j_next, m1, m2), m1, m2

    j_first = qi - DIFF_GROUP
    lax.while_loop(cond, body, (j_first, more(j_first, m1, m2), m1, m2))

    lam = lam_ref[...]
    lam_full = (jnp.exp(jnp.sum(lam[0:1] * lam[1:2], keepdims=True))
                - jnp.exp(jnp.sum(lam[2:3] * lam[3:4], keepdims=True)) + lam_init)
    a1 = acc_ref[0]
    a2 = acc_ref[1]
    out_t = (a1[0:LANES] / a1[LANES:LANES + 1]
             - lam_full * (a2[0:LANES] / a2[LANES:LANES + 1]))
    out = out_t.T
    ms = jnp.mean(out * out, axis=-1, keepdims=True)
    out = out * lax.rsqrt(ms + RMS_EPS) * g_ref[...] * (1.0 - lam_init)
    o_ref[0] = out.astype(o_ref.dtype)


def _diff_attention(u, d, lam, g_sub, lam_init):
    b, s, _ = u.shape
    t = ATT_TILE
    n_heads = d // LANES
    return pl.pallas_call(
        functools.partial(_diff_kernel, t=t, n_heads=n_heads, lam_init=lam_init),
        grid=(b, n_heads, s // t),
        in_specs=[
            pl.BlockSpec((1, t, LANES), lambda i, h, j: (i, j, h)),
            pl.BlockSpec((1, s, LANES), lambda i, h, j: (i, 0, n_heads + h)),
            pl.BlockSpec((1, s, LANES), lambda i, h, j: (i, 0, 2 * n_heads + h)),
            pl.BlockSpec(lam.shape, lambda i, h, j: (0, 0)),
            pl.BlockSpec((1, LANES), lambda i, h, j: (0, 0)),
        ],
        out_specs=pl.BlockSpec((1, t, LANES), lambda i, h, j: (i, j, h)),
        out_shape=jax.ShapeDtypeStruct((b, s, d), BF16),
        scratch_shapes=[pltpu.VMEM((s // t, LANES + ROWS_ONES, t), BF16),
                        pltpu.VMEM((s // t, SUBLANES, LANES), F32),
                        pltpu.VMEM((2, LANES + ROWS_ONES, t), F32)],
        compiler_params=pltpu.CompilerParams(
            dimension_semantics=("arbitrary", "arbitrary", "arbitrary"),
            vmem_limit_bytes=VMEM_LIMIT),
        name="diff_attn",
    )(u, u, u, lam, g_sub.reshape(1, LANES))


def kernel(x, c, w_ada, b_ada, g_pre, g_post, w_in, w_out, diff_lambda, diff_subln):
    depth = w_ada.shape[0]
    d = x.shape[-1]
    assert w_in.shape[2] == 4 * d and d % LANES == 0
    assert x.shape[1] % ROW_TILE == 0 and x.shape[1] % ATT_TILE == 0 and ATT_TILE % CHUNK == 0
    ada = _ada(c, w_ada, b_ada)
    w_in_b = w_in.astype(BF16)
    w_out_b = w_out.astype(BF16)
    for i in range(depth):
        u = _pre(x, g_pre[i], ada[i], w_in_b[i])
        if i % N_MIXERS == 0:
            y = _sb_attention(u, d)
        else:
            j = i // N_MIXERS
            y = _diff_attention(u, d, diff_lambda[j], diff_subln[j], _diff_lambda_init(i))
        x = _post(y, u, x, g_post[i], ada[i], w_out_b[i])
    return x
```

```python
import functools
import math

import jax
import jax.numpy as jnp
from jax import lax
from jax.experimental import pallas as pl
from jax.experimental.pallas import tpu as pltpu

F32 = jnp.float32
BF16 = jnp.bfloat16

LANES = 128
SUBLANES = 8
N_MIXERS = 2
CHUNK = 64
SB_HEAD_DIM = 64
DIFF_HEAD_DIM = 64
ALIBI_MAX_EXP = 8.0
RMS_EPS = 1e-6
NEG_INF = -1e30
LOG2E = 1.4426950408889634
SB_LOGIT_CAP = 126.0
SB_EXIT_BITS = 160.0
DIFF_GROUP = 4
EXP_ZERO_BELOW = 104.0
FAST_BELOW = 60.0
BOUND_SLACK = 1.05
ROWS_ONES = 16

ROW_TILE = 512
ATT_TILE = 256
VMEM_LIMIT = 48 * 1024 * 1024

_NT = (((1,), (1,)), ((), ()))


def _diff_lambda_init(layer_idx):
    return 0.8 - 0.6 * math.exp(-0.3 * layer_idx)


def _ada_kernel(c_ref, w_ref, b_ref, o_ref):
    c = c_ref[...]
    cond = c * jax.nn.sigmoid(c)
    o_ref[0] = jnp.dot(cond, w_ref[0], preferred_element_type=F32,
                       precision=lax.Precision.HIGHEST) + b_ref[0]


def _ada(c, w_ada, b_ada):
    depth, d, d3 = w_ada.shape
    nb = c.shape[0]
    b = -(-nb // SUBLANES) * SUBLANES
    c = jnp.pad(c, ((0, b - nb), (0, 0)))
    nt = d3 // d
    out = pl.pallas_call(
        _ada_kernel,
        grid=(depth, nt),
        in_specs=[
            pl.BlockSpec((b, d), lambda i, j: (0, 0)),
            pl.BlockSpec((1, d, d), lambda i, j: (i, 0, j)),
            pl.BlockSpec((1, 1, d), lambda i, j: (i, 0, j)),
        ],
        out_specs=pl.BlockSpec((1, b, d), lambda i, j: (i, 0, j)),
        out_shape=jax.ShapeDtypeStruct((depth, b, d3), F32),
        name="ada",
    )(c, w_ada, b_ada.reshape(depth, 1, d3))
    return out[:, :nb]


def _pre_kernel(x_ref, g_ref, ada_ref, w_ref, u_ref, *, d):
    x = x_ref[0]
    ms = jnp.mean(x * x, axis=-1, keepdims=True)
    y = x * lax.rsqrt(ms + RMS_EPS) * g_ref[...]
    ada = ada_ref[0]
    h = (y * (1.0 + ada[:, d:2 * d]) + ada[:, 0:d]).astype(BF16)
    n_out = w_ref.shape[1]
    for n in range(n_out // d):
        u_ref[0, :, n * d:(n + 1) * d] = jnp.dot(
            h, w_ref[:, n * d:(n + 1) * d], preferred_element_type=F32).astype(BF16)


def _pre(x, g_pre, ada, w_in):
    b, s, d = x.shape
    n_out = w_in.shape[1]
    tm = ROW_TILE
    return pl.pallas_call(
        functools.partial(_pre_kernel, d=d),
        grid=(b, s // tm),
        in_specs=[
            pl.BlockSpec((1, tm, d), lambda i, j: (i, j, 0)),
            pl.BlockSpec((1, d), lambda i, j: (0, 0)),
            pl.BlockSpec((1, 1, 3 * d), lambda i, j: (i, 0, 0)),
            pl.BlockSpec((d, n_out), lambda i, j: (0, 0)),
        ],
        out_specs=pl.BlockSpec((1, tm, n_out), lambda i, j: (i, j, 0)),
        out_shape=jax.ShapeDtypeStruct((b, s, n_out), BF16),
        compiler_params=pltpu.CompilerParams(
            dimension_semantics=("arbitrary", "arbitrary"), vmem_limit_bytes=VMEM_LIMIT),
        name="pre",
    )(x, g_pre.reshape(1, d), ada.reshape(b, 1, 3 * d), w_in)


def _post_kernel(y_ref, z_ref, x_ref, g_ref, ada_ref, w_ref, o_ref, *, d):
    y = y_ref[0].astype(F32)
    z = z_ref[0].astype(F32)
    t = (y * (z * jax.nn.sigmoid(z))).astype(BF16)
    r = jnp.dot(t, w_ref[...], preferred_element_type=F32)
    ms = jnp.mean(r * r, axis=-1, keepdims=True)
    rn = r * lax.rsqrt(ms + RMS_EPS) * g_ref[...]
    gate = ada_ref[0][:, 2 * d:3 * d]
    o_ref[0] = x_ref[0] + gate * rn


def _post(y, u, x, g_post, ada, w_out):
    b, s, d = x.shape
    tm = ROW_TILE
    z_col = u.shape[2] // d - 1
    return pl.pallas_call(
        functools.partial(_post_kernel, d=d),
        grid=(b, s // tm),
        in_specs=[
            pl.BlockSpec((1, tm, d), lambda i, j: (i, j, 0)),
            pl.BlockSpec((1, tm, d), lambda i, j: (i, j, z_col)),
            pl.BlockSpec((1, tm, d), lambda i, j: (i, j, 0)),
            pl.BlockSpec((1, d), lambda i, j: (0, 0)),
            pl.BlockSpec((1, 1, 3 * d), lambda i, j: (i, 0, 0)),
            pl.BlockSpec((d, d), lambda i, j: (0, 0)),
        ],
        out_specs=pl.BlockSpec((1, tm, d), lambda i, j: (i, j, 0)),
        out_shape=jax.ShapeDtypeStruct((b, s, d), F32),
        compiler_params=pltpu.CompilerParams(
            dimension_semantics=("arbitrary", "arbitrary"), vmem_limit_bytes=VMEM_LIMIT),
        name="post",
    )(y, u, x, g_post.reshape(1, d), ada.reshape(b, 1, 3 * d), w_out)


def _sb_kernel(q_ref, k_ref, v_ref, o_ref, acc_ref, *, t):
    qi = pl.program_id(2)
    q = q_ref[0].astype(F32) * ((SB_HEAD_DIM ** -0.5) * LOG2E)
    lane = lax.broadcasted_iota(jnp.int32, (t, LANES), 1)
    q_lo = jnp.where(lane < SB_HEAD_DIM, q, 0.0).astype(BF16)
    q_hi = jnp.where(lane < SB_HEAD_DIM, 0.0, q).astype(BF16)
    row = lax.broadcasted_iota(jnp.int32, (t, t), 0)
    col = lax.broadcasted_iota(jnp.int32, (t, t), 1)
    strict = col < row
    tri = jnp.where(row > col, 1.0, 0.0).astype(BF16)

    def chain(qm, k2, v2, carry, diag):
        z = lax.dot_general(qm, k2, _NT, preferred_element_type=F32)
        z = jnp.minimum(z, SB_LOGIT_CAP)
        drop = jnp.log2(1.0 + jnp.exp2(z))
        log_beta = z - drop
        if diag:
            drop = jnp.where(strict, drop, 0.0)
        between = jnp.dot(drop.astype(BF16), tri, preferred_element_type=F32)
        w = jnp.exp2(log_beta - between - carry)
        if diag:
            w = jnp.where(strict, w, 0.0)
        pv = jnp.dot(w.astype(BF16), v2, preferred_element_type=F32)
        return carry + between[:, 0:1] + drop[:, 0:1], pv

    def key_value(j):
        rows = pl.ds(pl.multiple_of(j * t, t), t)
        return k_ref[0, rows, :], v_ref[0, rows, :]

    kd, vd = key_value(qi)
    kp, vp = key_value(jnp.maximum(qi - 1, 0))
    has_prev = jnp.full((t, 1), qi, jnp.int32) >= 1
    carries = []
    for idx, qm in enumerate((q_lo, q_hi)):
        c, pv_d = chain(qm, kd, vd, jnp.zeros((t, 1), F32), True)
        c, pv_p = chain(qm, kp, vp, jnp.where(has_prev, c, -NEG_INF), False)
        acc_ref[idx] = pv_d + pv_p
        carries.append(c)

    def live(c_lo, c_hi):
        return jnp.min(jnp.minimum(c_lo, c_hi)) < SB_EXIT_BITS

    def cond(c):
        return jnp.logical_and(c[0] >= 0, c[1])

    def body(c):
        j, _, c_lo, c_hi = c
        k2, v2 = key_value(j)
        c_lo, pv_lo = chain(q_lo, k2, v2, c_lo, False)
        c_hi, pv_hi = chain(q_hi, k2, v2, c_hi, False)
        acc_ref[0] += pv_lo
        acc_ref[1] += pv_hi
        return j - 1, live(c_lo, c_hi), c_lo, c_hi

    lax.while_loop(cond, body, (qi - 2, live(*carries), *carries))
    o_ref[0] = jnp.where(lane < SB_HEAD_DIM, acc_ref[0], acc_ref[1]).astype(o_ref.dtype)


def _sb_attention(u, d):
    b, s, _ = u.shape
    t = ATT_TILE
    npair = d // LANES
    return pl.pallas_call(
        functools.partial(_sb_kernel, t=t),
        grid=(b, npair, s // t),
        in_specs=[
            pl.BlockSpec((1, t, LANES), lambda i, h, j: (i, j, h)),
            pl.BlockSpec((1, s, LANES), lambda i, h, j: (i, 0, npair + h)),
            pl.BlockSpec((1, s, LANES), lambda i, h, j: (i, 0, 2 * npair + h)),
        ],
        out_specs=pl.BlockSpec((1, t, LANES), lambda i, h, j: (i, j, h)),
        out_shape=jax.ShapeDtypeStruct((b, s, d), BF16),
        scratch_shapes=[pltpu.VMEM((2, t, LANES), F32)],
        compiler_params=pltpu.CompilerParams(
            dimension_semantics=("arbitrary", "arbitrary", "arbitrary"),
            vmem_limit_bytes=VMEM_LIMIT),
        name="sb_attn",
    )(u, u, u)


def _diff_kernel(q_ref, k_ref, v_ref, lam_ref, g_ref, o_ref, vt_ref, kmax_ref, acc_ref,
                 *, t, n_heads, lam_init):
    h = pl.program_id(1)
    qi = pl.program_id(2)
    n_tiles = k_ref.shape[1] // t
    head_no = (lax.broadcasted_iota(jnp.int32, (1, 1), 0) + (h + 1)).astype(F32)
    slope = jnp.exp2(-head_no * (ALIBI_MAX_EXP / n_heads))
    lane = lax.broadcasted_iota(jnp.int32, (t, LANES), 1)
    half = lax.broadcasted_iota(jnp.int32, (SUBLANES, LANES), 0)
    feat = lax.broadcasted_iota(jnp.int32, (SUBLANES, LANES), 1)
    pick = jnp.where((feat >= DIFF_HEAD_DIM) == (half == 1), 1.0, 0.0)
    pick = jnp.where(half < 2, pick, 0.0).astype(BF16)

    def sq_norms(x):
        xf = x.astype(F32)
        return lax.dot_general(pick, (xf * xf).astype(BF16), _NT, preferred_element_type=F32)

    @pl.when(qi == 0)
    def _():
        eye = jnp.where(lax.broadcasted_iota(jnp.int32, (LANES, LANES), 0)
                        == lax.broadcasted_iota(jnp.int32, (LANES, LANES), 1), 1.0, 0.0).astype(BF16)

        def fill(n, kmax):
            rows = pl.ds(pl.multiple_of(n * t, t), t)
            vt = lax.dot_general(eye, v_ref[0, rows, :], _NT, preferred_element_type=F32)
            vt_ref[n, 0:LANES, :] = vt.astype(BF16)
            vt_ref[n, LANES:LANES + ROWS_ONES, :] = jnp.ones((ROWS_ONES, t), BF16)
            kmax = jnp.maximum(kmax, jnp.max(sq_norms(k_ref[0, rows, :]), axis=1, keepdims=True))
            kmax_ref[n] = jnp.broadcast_to(kmax, (SUBLANES, LANES))
            return kmax

        lax.fori_loop(0, n_tiles, fill, jnp.zeros((SUBLANES, 1), F32))

    sub = lax.broadcasted_iota(jnp.int32, (t, LANES), 0)
    q_raw = q_ref[0]
    q = q_raw.astype(F32) * (DIFF_HEAD_DIM ** -0.5)
    one_col = jnp.where(lane == 0, 1.0, 0.0).astype(BF16)
    q1 = jnp.concatenate([jnp.where(lane < DIFF_HEAD_DIM, q, 0.0).astype(BF16), one_col], axis=1)
    q2 = jnp.concatenate([jnp.where(lane < DIFF_HEAD_DIM, 0.0, q).astype(BF16), one_col], axis=1)
    k_aug = jnp.where(lane == 0, slope * sub.astype(F32), 0.0).astype(BF16)
    q_pos = lax.broadcasted_iota(jnp.int32, (1, t), 1)
    q_sq = sq_norms(q_raw) * (BOUND_SLACK / DIFF_HEAD_DIM)

    def key_tile(j):
        k2 = k_ref[0, pl.ds(pl.multiple_of(j * t, t), t), :]
        return jnp.concatenate([k2, k_aug], axis=1)

    def query_bias(j):
        return -slope * (q_pos + (qi - j) * t).astype(F32)

    def update(idx, scores, biases, vt, m):
        m_new = m
        for s, c in zip(scores, biases):
            m_new = jnp.maximum(m_new, jnp.max(s, axis=0, keepdims=True) + c)
        alpha = jnp.exp(m - m_new)
        p = jnp.concatenate([jnp.exp(s - (m_new - c)).astype(BF16) for s, c in zip(scores, biases)], axis=0)
        acc_ref[idx] = alpha * acc_ref[idx] + jnp.dot(vt, p, preferred_element_type=F32)
        return m_new

    def accumulate(idx, scores, biases, vt, m):
        p = jnp.concatenate([jnp.exp(s - (m - c)).astype(BF16) for s, c in zip(scores, biases)], axis=0)
        acc_ref[idx] += jnp.dot(vt, p, preferred_element_type=F32)
        return m

    def group(j_top, m1, m2, diag_bias, step):
        js = [jnp.maximum(j_top - g, 0) for g in range(DIFF_GROUP)]
        cs = [jnp.where(jnp.full((1, t), j_top - g, jnp.int32) >= 0, query_bias(js[g]), NEG_INF)
              for g in range(DIFF_GROUP)]
        ks = [key_tile(j) for j in js]
        vt = jnp.concatenate([vt_ref[j] for j in js], axis=1)
        s1 = [lax.dot_general(k, q1, _NT, preferred_element_type=F32) for k in ks]
        s2 = [lax.dot_general(k, q2, _NT, preferred_element_type=F32) for k in ks]
        if diag_bias is not None:
            s1[0] = s1[0] + diag_bias
            s2[0] = s2[0] + diag_bias
            cs[0] = jnp.zeros((1, t), F32)
        return step(0, s1, cs, vt, m1), step(1, s2, cs, vt, m2)

    def headroom(j_top, m1, m2):
        k_sq = kmax_ref[jnp.maximum(j_top, 0)]
        reach = jnp.sqrt(q_sq * k_sq[:, 0:1])
        nearest = -slope * (q_pos + ((qi - j_top - 1) * t + 1)).astype(F32)
        return jnp.max(jnp.maximum(reach[0:1] - m1, reach[1:2] - m2) + nearest)

    kpos = lax.broadcasted_iota(jnp.int32, (t, t), 0)
    qpos = lax.broadcasted_iota(jnp.int32, (t, t), 1)
    diag_bias = jnp.where(kpos // CHUNK <= qpos // CHUNK,
                          -slope * (jnp.abs(qpos - kpos) + kpos).astype(F32), NEG_INF)
    acc_ref[...] = jnp.zeros(acc_ref.shape, F32)
    m0 = jnp.full((1, t), NEG_INF, F32)
    m1, m2 = group(qi, m0, m0, diag_bias, update)

    def cond(c):
        return jnp.logical_and(c[0] >= 0, c[1] > -EXP_ZERO_BELOW)

    def body(c):
        j_top, room, m1, m2 = c
        j_next = j_top - DIFF_GROUP

        def fast():
            return (headroom(j_next, m1, m2),) + group(j_top, m1, m2, None, accumulate)

        def exact():
            return (headroom(j_next, m1, m2),) + group(j_top, m1, m2, None, update)

        room, m1, m2 = lax.cond(room < FAST_BELOW, fast, exact)
        return j_next, room, m1, m2

    j_first = qi - DIFF_GROUP
    lax.while_loop(cond, body, (j_first, headroom(j_first, m1, m2), m1, m2))

    lam = lam_ref[...]
    lam_full = (jnp.exp(jnp.sum(lam[0:1] * lam[1:2], keepdims=True))
                - jnp.exp(jnp.sum(lam[2:3] * lam[3:4], keepdims=True)) + lam_init)
    a1 = acc_ref[0]
    a2 = acc_ref[1]
    out_t = (a1[0:LANES] / a1[LANES:LANES + 1]
             - lam_full * (a2[0:LANES] / a2[LANES:LANES + 1]))
    out = out_t.T
    ms = jnp.mean(out * out, axis=-1, keepdims=True)
    out = out * lax.rsqrt(ms + RMS_EPS) * g_ref[...] * (1.0 - lam_init)
    o_ref[0] = out.astype(o_ref.dtype)


def _diff_attention(u, d, lam, g_sub, lam_init):
    b, s, _ = u.shape
    t = ATT_TILE
    n_heads = d // LANES
    return pl.pallas_call(
        functools.partial(_diff_kernel, t=t, n_heads=n_heads, lam_init=lam_init),
        grid=(b, n_heads, s // t),
        in_specs=[
            pl.BlockSpec((1, t, LANES), lambda i, h, j: (i, j, h)),
            pl.BlockSpec((1, s, LANES), lambda i, h, j: (i, 0, n_heads + h)),
            pl.BlockSpec((1, s, LANES), lambda i, h, j: (i, 0, 2 * n_heads + h)),
            pl.BlockSpec(lam.shape, lambda i, h, j: (0, 0)),
            pl.BlockSpec((1, LANES), lambda i, h, j: (0, 0)),
        ],
        out_specs=pl.BlockSpec((1, t, LANES), lambda i, h, j: (i, j, h)),
        out_shape=jax.ShapeDtypeStruct((b, s, d), BF16),
        scratch_shapes=[pltpu.VMEM((s // t, LANES + ROWS_ONES, t), BF16),
                        pltpu.VMEM((s // t, SUBLANES, LANES), F32),
                        pltpu.VMEM((2, LANES + ROWS_ONES, t), F32)],
        compiler_params=pltpu.CompilerParams(
            dimension_semantics=("arbitrary", "arbitrary", "arbitrary"),
            vmem_limit_bytes=VMEM_LIMIT),
        name="diff_attn",
    )(u, u, u, lam, g_sub.reshape(1, LANES))


def kernel(x, c, w_ada, b_ada, g_pre, g_post, w_in, w_out, diff_lambda, diff_subln):
    depth = w_ada.shape[0]
    d = x.shape[-1]
    assert w_in.shape[2] == 4 * d and d % LANES == 0
    assert x.shape[1] % ROW_TILE == 0 and x.shape[1] % ATT_TILE == 0 and ATT_TILE % CHUNK == 0
    ada = _ada(c, w_ada, b_ada)
    w_in_b = w_in.astype(BF16)
    w_out_b = w_out.astype(BF16)
    for i in range(depth):
        u = _pre(x, g_pre[i], ada[i], w_in_b[i])
        if i % N_MIXERS == 0:
            y = _sb_attention(u, d)
        else:
            j = i // N_MIXERS
            y = _diff_attention(u, d, diff_lambda[j], diff_subln[j], _diff_lambda_init(i))
        x = _post(y, u, x, g_post[i], ada[i], w_out_b[i])
    return x
```

```python
import functools
import math

import jax
import jax.numpy as jnp
from jax import lax
from jax.experimental import pallas as pl
from jax.experimental.pallas import tpu as pltpu

F32 = jnp.float32
BF16 = jnp.bfloat16

LANES = 128
SUBLANES = 8
N_MIXERS = 2
CHUNK = 64
SB_HEAD_DIM = 64
DIFF_HEAD_DIM = 64
ALIBI_MAX_EXP = 8.0
RMS_EPS = 1e-6
NEG_INF = -1e30
LOG2E = 1.4426950408889634
SB_LOGIT_CAP = 126.0
SB_EXIT_BITS = 160.0
DIFF_GROUP = 4
EXP_ZERO_BELOW = 104.0
FAST_BELOW = 60.0
BOUND_SLACK = 1.05
ROWS_ONES = 16

ROW_TILE = 512
ATT_TILE = 256
DIFF_Q_TILE = 512
VMEM_LIMIT = 48 * 1024 * 1024

_NT = (((1,), (1,)), ((), ()))


def _diff_lambda_init(layer_idx):
    return 0.8 - 0.6 * math.exp(-0.3 * layer_idx)


def _ada_kernel(c_ref, w_ref, b_ref, o_ref):
    c = c_ref[...]
    cond = c * jax.nn.sigmoid(c)
    o_ref[0] = jnp.dot(cond, w_ref[0], preferred_element_type=F32,
                       precision=lax.Precision.HIGHEST) + b_ref[0]


def _ada(c, w_ada, b_ada):
    depth, d, d3 = w_ada.shape
    nb = c.shape[0]
    b = -(-nb // SUBLANES) * SUBLANES
    c = jnp.pad(c, ((0, b - nb), (0, 0)))
    nt = d3 // d
    out = pl.pallas_call(
        _ada_kernel,
        grid=(depth, nt),
        in_specs=[
            pl.BlockSpec((b, d), lambda i, j: (0, 0)),
            pl.BlockSpec((1, d, d), lambda i, j: (i, 0, j)),
            pl.BlockSpec((1, 1, d), lambda i, j: (i, 0, j)),
        ],
        out_specs=pl.BlockSpec((1, b, d), lambda i, j: (i, 0, j)),
        out_shape=jax.ShapeDtypeStruct((depth, b, d3), F32),
        name="ada",
    )(c, w_ada, b_ada.reshape(depth, 1, d3))
    return out[:, :nb]


def _pre_kernel(x_ref, g_ref, ada_ref, w_ref, u_ref, *, d):
    x = x_ref[0]
    ms = jnp.mean(x * x, axis=-1, keepdims=True)
    y = x * lax.rsqrt(ms + RMS_EPS) * g_ref[...]
    ada = ada_ref[0]
    h = (y * (1.0 + ada[:, d:2 * d]) + ada[:, 0:d]).astype(BF16)
    n_out = w_ref.shape[1]
    for n in range(n_out // d):
        u_ref[0, :, n * d:(n + 1) * d] = jnp.dot(
            h, w_ref[:, n * d:(n + 1) * d], preferred_element_type=F32).astype(BF16)


def _pre(x, g_pre, ada, w_in):
    b, s, d = x.shape
    n_out = w_in.shape[1]
    tm = ROW_TILE
    return pl.pallas_call(
        functools.partial(_pre_kernel, d=d),
        grid=(b, s // tm),
        in_specs=[
            pl.BlockSpec((1, tm, d), lambda i, j: (i, j, 0)),
            pl.BlockSpec((1, d), lambda i, j: (0, 0)),
            pl.BlockSpec((1, 1, 3 * d), lambda i, j: (i, 0, 0)),
            pl.BlockSpec((d, n_out), lambda i, j: (0, 0)),
        ],
        out_specs=pl.BlockSpec((1, tm, n_out), lambda i, j: (i, j, 0)),
        out_shape=jax.ShapeDtypeStruct((b, s, n_out), BF16),
        compiler_params=pltpu.CompilerParams(
            dimension_semantics=("arbitrary", "arbitrary"), vmem_limit_bytes=VMEM_LIMIT),
        name="pre",
    )(x, g_pre.reshape(1, d), ada.reshape(b, 1, 3 * d), w_in)


def _post_kernel(y_ref, z_ref, x_ref, g_ref, ada_ref, w_ref, o_ref, *, d):
    y = y_ref[0].astype(F32)
    z = z_ref[0].astype(F32)
    t = (y * (z * jax.nn.sigmoid(z))).astype(BF16)
    r = jnp.dot(t, w_ref[...], preferred_element_type=F32)
    ms = jnp.mean(r * r, axis=-1, keepdims=True)
    rn = r * lax.rsqrt(ms + RMS_EPS) * g_ref[...]
    gate = ada_ref[0][:, 2 * d:3 * d]
    o_ref[0] = x_ref[0] + gate * rn


def _post(y, u, x, g_post, ada, w_out):
    b, s, d = x.shape
    tm = ROW_TILE
    z_col = u.shape[2] // d - 1
    return pl.pallas_call(
        functools.partial(_post_kernel, d=d),
        grid=(b, s // tm),
        in_specs=[
            pl.BlockSpec((1, tm, d), lambda i, j: (i, j, 0)),
            pl.BlockSpec((1, tm, d), lambda i, j: (i, j, z_col)),
            pl.BlockSpec((1, tm, d), lambda i, j: (i, j, 0)),
            pl.BlockSpec((1, d), lambda i, j: (0, 0)),
            pl.BlockSpec((1, 1, 3 * d), lambda i, j: (i, 0, 0)),
            pl.BlockSpec((d, d), lambda i, j: (0, 0)),
        ],
        out_specs=pl.BlockSpec((1, tm, d), lambda i, j: (i, j, 0)),
        out_shape=jax.ShapeDtypeStruct((b, s, d), F32),
        compiler_params=pltpu.CompilerParams(
            dimension_semantics=("arbitrary", "arbitrary"), vmem_limit_bytes=VMEM_LIMIT),
        name="post",
    )(y, u, x, g_post.reshape(1, d), ada.reshape(b, 1, 3 * d), w_out)


def _sb_kernel(q_ref, k_ref, v_ref, o_ref, acc_ref, *, t):
    qi = pl.program_id(2)
    q = q_ref[0].astype(F32) * ((SB_HEAD_DIM ** -0.5) * LOG2E)
    lane = lax.broadcasted_iota(jnp.int32, (t, LANES), 1)
    q_lo = jnp.where(lane < SB_HEAD_DIM, q, 0.0).astype(BF16)
    q_hi = jnp.where(lane < SB_HEAD_DIM, 0.0, q).astype(BF16)
    row = lax.broadcasted_iota(jnp.int32, (t, t), 0)
    col = lax.broadcasted_iota(jnp.int32, (t, t), 1)
    strict = col < row
    tri = jnp.where(row > col, 1.0, 0.0).astype(BF16)

    def chain(qm, k2, v2, carry, diag):
        z = lax.dot_general(qm, k2, _NT, preferred_element_type=F32)
        z = jnp.minimum(z, SB_LOGIT_CAP)
        drop = jnp.log2(1.0 + jnp.exp2(z))
        log_beta = z - drop
        if diag:
            drop = jnp.where(strict, drop, 0.0)
        between = jnp.dot(drop.astype(BF16), tri, preferred_element_type=F32)
        w = jnp.exp2(log_beta - between - carry)
        if diag:
            w = jnp.where(strict, w, 0.0)
        pv = jnp.dot(w.astype(BF16), v2, preferred_element_type=F32)
        return carry + between[:, 0:1] + drop[:, 0:1], pv

    def key_value(j):
        rows = pl.ds(pl.multiple_of(j * t, t), t)
        return k_ref[0, rows, :], v_ref[0, rows, :]

    kd, vd = key_value(qi)
    kp, vp = key_value(jnp.maximum(qi - 1, 0))
    has_prev = jnp.full((t, 1), qi, jnp.int32) >= 1
    carries = []
    for idx, qm in enumerate((q_lo, q_hi)):
        c, pv_d = chain(qm, kd, vd, jnp.zeros((t, 1), F32), True)
        c, pv_p = chain(qm, kp, vp, jnp.where(has_prev, c, -NEG_INF), False)
        acc_ref[idx] = pv_d + pv_p
        carries.append(c)

    def live(c_lo, c_hi):
        return jnp.min(jnp.minimum(c_lo, c_hi)) < SB_EXIT_BITS

    def cond(c):
        return jnp.logical_and(c[0] >= 0, c[1])

    def body(c):
        j, _, c_lo, c_hi = c
        k2, v2 = key_value(j)
        c_lo, pv_lo = chain(q_lo, k2, v2, c_lo, False)
        c_hi, pv_hi = chain(q_hi, k2, v2, c_hi, False)
        acc_ref[0] += pv_lo
        acc_ref[1] += pv_hi
        return j - 1, live(c_lo, c_hi), c_lo, c_hi

    lax.while_loop(cond, body, (qi - 2, live(*carries), *carries))
    o_ref[0] = jnp.where(lane < SB_HEAD_DIM, acc_ref[0], acc_ref[1]).astype(o_ref.dtype)


def _sb_attention(u, d):
    b, s, _ = u.shape
    t = ATT_TILE
    npair = d // LANES
    return pl.pallas_call(
        functools.partial(_sb_kernel, t=t),
        grid=(b, npair, s // t),
        in_specs=[
            pl.BlockSpec((1, t, LANES), lambda i, h, j: (i, j, h)),
            pl.BlockSpec((1, s, LANES), lambda i, h, j: (i, 0, npair + h)),
            pl.BlockSpec((1, s, LANES), lambda i, h, j: (i, 0, 2 * npair + h)),
        ],
        out_specs=pl.BlockSpec((1, t, LANES), lambda i, h, j: (i, j, h)),
        out_shape=jax.ShapeDtypeStruct((b, s, d), BF16),
        scratch_shapes=[pltpu.VMEM((2, t, LANES), F32)],
        compiler_params=pltpu.CompilerParams(
            dimension_semantics=("arbitrary", "arbitrary", "arbitrary"),
            vmem_limit_bytes=VMEM_LIMIT),
        name="sb_attn",
    )(u, u, u)


def _diff_kernel(q_ref, k_ref, v_ref, lam_ref, g_ref, o_ref, vt_ref, kmax_ref, bias_ref, acc_ref,
                 *, t, tq, n_heads, lam_init):
    h = pl.program_id(1)
    qi = pl.program_id(2)
    n_tiles = k_ref.shape[1] // t
    head_no = (lax.broadcasted_iota(jnp.int32, (1, 1), 0) + (h + 1)).astype(F32)
    slope = jnp.exp2(-head_no * (ALIBI_MAX_EXP / n_heads))
    lane = lax.broadcasted_iota(jnp.int32, (tq, LANES), 1)
    n_diag = tq // t
    half = lax.broadcasted_iota(jnp.int32, (SUBLANES, LANES), 0)
    feat = lax.broadcasted_iota(jnp.int32, (SUBLANES, LANES), 1)
    pick = jnp.where((feat >= DIFF_HEAD_DIM) == (half == 1), 1.0, 0.0)
    pick = jnp.where(half < 2, pick, 0.0).astype(BF16)

    def sq_norms(x):
        xf = x.astype(F32)
        return lax.dot_general(pick, (xf * xf).astype(BF16), _NT, preferred_element_type=F32)

    @pl.when(qi == 0)
    def _():
        eye = jnp.where(lax.broadcasted_iota(jnp.int32, (LANES, LANES), 0)
                        == lax.broadcasted_iota(jnp.int32, (LANES, LANES), 1), 1.0, 0.0).astype(BF16)

        def fill(n, kmax):
            rows = pl.ds(pl.multiple_of(n * t, t), t)
            vt = lax.dot_general(eye, v_ref[0, rows, :], _NT, preferred_element_type=F32)
            vt_ref[n, 0:LANES, :] = vt.astype(BF16)
            vt_ref[n, LANES:LANES + ROWS_ONES, :] = jnp.ones((ROWS_ONES, t), BF16)
            kmax = jnp.maximum(kmax, jnp.max(sq_norms(k_ref[0, rows, :]), axis=1, keepdims=True))
            kmax_ref[n] = jnp.broadcast_to(kmax, (SUBLANES, LANES))
            return kmax

        lax.fori_loop(0, n_tiles, fill, jnp.zeros((SUBLANES, 1), F32))
        qpos = lax.broadcasted_iota(jnp.int32, (t, tq), 1)
        for d in range(n_diag):
            kpos = lax.broadcasted_iota(jnp.int32, (t, tq), 0) + d * t
            ahead = jnp.maximum(kpos - qpos, 0).astype(F32)
            bias_ref[d] = jnp.where(kpos // CHUNK <= qpos // CHUNK, -2.0 * slope * ahead, NEG_INF)

    sub = lax.broadcasted_iota(jnp.int32, (t, LANES), 0)
    q_raw = q_ref[0]
    q = q_raw.astype(F32) * (DIFF_HEAD_DIM ** -0.5)
    one_col = jnp.where(lane == 0, 1.0, 0.0).astype(BF16)
    q1 = jnp.concatenate([jnp.where(lane < DIFF_HEAD_DIM, q, 0.0).astype(BF16), one_col], axis=1)
    q2 = jnp.concatenate([jnp.where(lane < DIFF_HEAD_DIM, 0.0, q).astype(BF16), one_col], axis=1)
    k_aug = jnp.where(lax.broadcasted_iota(jnp.int32, (t, LANES), 1) == 0,
                      slope * sub.astype(F32), 0.0).astype(BF16)
    q_pos = lax.broadcasted_iota(jnp.int32, (1, tq), 1) + qi * tq
    q_sq = sq_norms(q_raw) * (BOUND_SLACK / DIFF_HEAD_DIM)

    def key_tile(j):
        k2 = k_ref[0, pl.ds(pl.multiple_of(j * t, t), t), :]
        return jnp.concatenate([k2, k_aug], axis=1)

    def query_bias(j):
        return -slope * (q_pos - j * t).astype(F32)

    def update(idx, scores, biases, vt, m):
        m_new = m
        for s, c in zip(scores, biases):
            m_new = jnp.maximum(m_new, jnp.max(s, axis=0, keepdims=True) + c)
        alpha = jnp.exp(m - m_new)
        p = jnp.concatenate([jnp.exp(s - (m_new - c)).astype(BF16) for s, c in zip(scores, biases)], axis=0)
        acc_ref[idx] = alpha * acc_ref[idx] + jnp.dot(vt, p, preferred_element_type=F32)
        return m_new

    def accumulate(idx, scores, biases, vt, m):
        p = jnp.concatenate([jnp.exp(s - (m - c)).astype(BF16) for s, c in zip(scores, biases)], axis=0)
        acc_ref[idx] += jnp.dot(vt, p, preferred_element_type=F32)
        return m

    def group(j_top, m1, m2, diagonal, step):
        js = [jnp.maximum(j_top - g, 0) for g in range(DIFF_GROUP)]
        cs = [jnp.where(jnp.full((1, tq), j_top - g, jnp.int32) >= 0, query_bias(js[g]), NEG_INF)
              for g in range(DIFF_GROUP)]
        ks = [key_tile(j) for j in js]
        vt = jnp.concatenate([vt_ref[j] for j in js], axis=1)
        s1 = [lax.dot_general(k, q1, _NT, preferred_element_type=F32) for k in ks]
        s2 = [lax.dot_general(k, q2, _NT, preferred_element_type=F32) for k in ks]
        if diagonal:
            for g in range(n_diag):
                s1[g] = s1[g] + bias_ref[n_diag - 1 - g]
                s2[g] = s2[g] + bias_ref[n_diag - 1 - g]
        return step(0, s1, cs, vt, m1), step(1, s2, cs, vt, m2)

    def headroom(j_top, m1, m2):
        k_sq = kmax_ref[jnp.maximum(j_top, 0)]
        reach = jnp.sqrt(q_sq * k_sq[:, 0:1])
        nearest = -slope * (q_pos - ((j_top + 1) * t - 1)).astype(F32)
        return jnp.max(jnp.maximum(reach[0:1] - m1, reach[1:2] - m2) + nearest)

    acc_ref[...] = jnp.zeros(acc_ref.shape, F32)
    m0 = jnp.full((1, tq), NEG_INF, F32)
    j_diag = (qi + 1) * n_diag - 1
    m1, m2 = group(j_diag, m0, m0, True, update)

    def cond(c):
        return jnp.logical_and(c[0] >= 0, c[1] > -EXP_ZERO_BELOW)

    def body(c):
        j_top, room, m1, m2 = c
        j_next = j_top - DIFF_GROUP

        def fast():
            return (headroom(j_next, m1, m2),) + group(j_top, m1, m2, False, accumulate)

        def exact():
            return (headroom(j_next, m1, m2),) + group(j_top, m1, m2, False, update)

        room, m1, m2 = lax.cond(room < FAST_BELOW, fast, exact)
        return j_next, room, m1, m2

    j_first = j_diag - DIFF_GROUP
    lax.while_loop(cond, body, (j_first, headroom(j_first, m1, m2), m1, m2))

    lam = lam_ref[...]
    lam_full = (jnp.exp(jnp.sum(lam[0:1] * lam[1:2], keepdims=True))
                - jnp.exp(jnp.sum(lam[2:3] * lam[3:4], keepdims=True)) + lam_init)
    a1 = acc_ref[0]
    a2 = acc_ref[1]
    out_t = (a1[0:LANES] / a1[LANES:LANES + 1]
             - lam_full * (a2[0:LANES] / a2[LANES:LANES + 1]))
    out = out_t.T
    ms = jnp.mean(out * out, axis=-1, keepdims=True)
    out = out * lax.rsqrt(ms + RMS_EPS) * g_ref[...] * (1.0 - lam_init)
    o_ref[0] = out.astype(o_ref.dtype)


def _diff_attention(u, d, lam, g_sub, lam_init):
    b, s, _ = u.shape
    t = ATT_TILE
    tq = DIFF_Q_TILE
    n_heads = d // LANES
    return pl.pallas_call(
        functools.partial(_diff_kernel, t=t, tq=tq, n_heads=n_heads, lam_init=lam_init),
        grid=(b, n_heads, s // tq),
        in_specs=[
            pl.BlockSpec((1, tq, LANES), lambda i, h, j: (i, j, h)),
            pl.BlockSpec((1, s, LANES), lambda i, h, j: (i, 0, n_heads + h)),
            pl.BlockSpec((1, s, LANES), lambda i, h, j: (i, 0, 2 * n_heads + h)),
            pl.BlockSpec(lam.shape, lambda i, h, j: (0, 0)),
            pl.BlockSpec((1, LANES), lambda i, h, j: (0, 0)),
        ],
        out_specs=pl.BlockSpec((1, tq, LANES), lambda i, h, j: (i, j, h)),
        out_shape=jax.ShapeDtypeStruct((b, s, d), BF16),
        scratch_shapes=[pltpu.VMEM((s // t, LANES + ROWS_ONES, t), BF16),
                        pltpu.VMEM((s // t, SUBLANES, LANES), F32),
                        pltpu.VMEM((tq // t, t, tq), F32),
                        pltpu.VMEM((2, LANES + ROWS_ONES, tq), F32)],
        compiler_params=pltpu.CompilerParams(
            dimension_semantics=("arbitrary", "arbitrary", "arbitrary"),
            vmem_limit_bytes=VMEM_LIMIT),
        name="diff_attn",
    )(u, u, u, lam, g_sub.reshape(1, LANES))


def kernel(x, c, w_ada, b_ada, g_pre, g_post, w_in, w_out, diff_lambda, diff_subln):
    depth = w_ada.shape[0]
    d = x.shape[-1]
    assert w_in.shape[2] == 4 * d and d % LANES == 0
    assert x.shape[1] % ROW_TILE == 0 and x.shape[1] % DIFF_Q_TILE == 0
    assert DIFF_Q_TILE % ATT_TILE == 0 and ATT_TILE % CHUNK == 0
    ada = _ada(c, w_ada, b_ada)
    w_in_b = w_in.astype(BF16)
    w_out_b = w_out.astype(BF16)
    for i in range(depth):
        u = _pre(x, g_pre[i], ada[i], w_in_b[i])
        if i % N_MIXERS == 0:
            y = _sb_attention(u, d)
        else:
            j = i // N_MIXERS
            y = _diff_attention(u, d, diff_lambda[j], diff_subln[j], _diff_lambda_init(i))
        x = _post(y, u, x, g_post[i], ada[i], w_out_b[i])
    return x
```

```python
import functools
import math

import jax
import jax.numpy as jnp
from jax import lax
from jax.experimental import pallas as pl
from jax.experimental.pallas import tpu as pltpu

F32 = jnp.float32
BF16 = jnp.bfloat16

LANES = 128
SUBLANES = 8
N_MIXERS = 2
CHUNK = 64
SB_HEAD_DIM = 64
DIFF_HEAD_DIM = 64
ALIBI_MAX_EXP = 8.0
RMS_EPS = 1e-6
NEG_INF = -1e30
LOG2E = 1.4426950408889634
SB_LOGIT_CAP = 126.0
SB_EXIT_BITS = 160.0
DIFF_GROUP = 4
EXP_ZERO_BELOW = 104.0
FAST_BELOW = 60.0
BOUND_SLACK = 1.05
ROWS_ONES = 16

ROW_TILE = 512
ATT_TILE = 256
SB_Q_TILES = 2
DIFF_Q_TILE = 512
VMEM_LIMIT = 48 * 1024 * 1024

_NT = (((1,), (1,)), ((), ()))


def _diff_lambda_init(layer_idx):
    return 0.8 - 0.6 * math.exp(-0.3 * layer_idx)


def _ada_kernel(c_ref, w_ref, b_ref, o_ref):
    c = c_ref[...]
    cond = c * jax.nn.sigmoid(c)
    o_ref[0] = jnp.dot(cond, w_ref[0], preferred_element_type=F32,
                       precision=lax.Precision.HIGHEST) + b_ref[0]


def _ada(c, w_ada, b_ada):
    depth, d, d3 = w_ada.shape
    nb = c.shape[0]
    b = -(-nb // SUBLANES) * SUBLANES
    c = jnp.pad(c, ((0, b - nb), (0, 0)))
    nt = d3 // d
    out = pl.pallas_call(
        _ada_kernel,
        grid=(depth, nt),
        in_specs=[
            pl.BlockSpec((b, d), lambda i, j: (0, 0)),
            pl.BlockSpec((1, d, d), lambda i, j: (i, 0, j)),
            pl.BlockSpec((1, 1, d), lambda i, j: (i, 0, j)),
        ],
        out_specs=pl.BlockSpec((1, b, d), lambda i, j: (i, 0, j)),
        out_shape=jax.ShapeDtypeStruct((depth, b, d3), F32),
        name="ada",
    )(c, w_ada, b_ada.reshape(depth, 1, d3))
    return out[:, :nb]


def _pre_kernel(x_ref, g_ref, ada_ref, w_ref, u_ref, *, d):
    x = x_ref[0]
    ms = jnp.mean(x * x, axis=-1, keepdims=True)
    y = x * lax.rsqrt(ms + RMS_EPS) * g_ref[...]
    ada = ada_ref[0]
    h = (y * (1.0 + ada[:, d:2 * d]) + ada[:, 0:d]).astype(BF16)
    n_out = w_ref.shape[1]
    for n in range(n_out // d):
        u_ref[0, :, n * d:(n + 1) * d] = jnp.dot(
            h, w_ref[:, n * d:(n + 1) * d], preferred_element_type=F32).astype(BF16)


def _pre(x, g_pre, ada, w_in):
    b, s, d = x.shape
    n_out = w_in.shape[1]
    tm = ROW_TILE
    return pl.pallas_call(
        functools.partial(_pre_kernel, d=d),
        grid=(b, s // tm),
        in_specs=[
            pl.BlockSpec((1, tm, d), lambda i, j: (i, j, 0)),
            pl.BlockSpec((1, d), lambda i, j: (0, 0)),
            pl.BlockSpec((1, 1, 3 * d), lambda i, j: (i, 0, 0)),
            pl.BlockSpec((d, n_out), lambda i, j: (0, 0)),
        ],
        out_specs=pl.BlockSpec((1, tm, n_out), lambda i, j: (i, j, 0)),
        out_shape=jax.ShapeDtypeStruct((b, s, n_out), BF16),
        compiler_params=pltpu.CompilerParams(
            dimension_semantics=("arbitrary", "arbitrary"), vmem_limit_bytes=VMEM_LIMIT),
        name="pre",
    )(x, g_pre.reshape(1, d), ada.reshape(b, 1, 3 * d), w_in)


def _post_kernel(y_ref, z_ref, x_ref, g_ref, ada_ref, w_ref, o_ref, *, d):
    y = y_ref[0].astype(F32)
    z = z_ref[0].astype(F32)
    t = (y * (z * jax.nn.sigmoid(z))).astype(BF16)
    r = jnp.dot(t, w_ref[...], preferred_element_type=F32)
    ms = jnp.mean(r * r, axis=-1, keepdims=True)
    rn = r * lax.rsqrt(ms + RMS_EPS) * g_ref[...]
    gate = ada_ref[0][:, 2 * d:3 * d]
    o_ref[0] = x_ref[0] + gate * rn


def _post(y, u, x, g_post, ada, w_out):
    b, s, d = x.shape
    tm = ROW_TILE
    z_col = u.shape[2] // d - 1
    return pl.pallas_call(
        functools.partial(_post_kernel, d=d),
        grid=(b, s // tm),
        in_specs=[
            pl.BlockSpec((1, tm, d), lambda i, j: (i, j, 0)),
            pl.BlockSpec((1, tm, d), lambda i, j: (i, j, z_col)),
            pl.BlockSpec((1, tm, d), lambda i, j: (i, j, 0)),
            pl.BlockSpec((1, d), lambda i, j: (0, 0)),
            pl.BlockSpec((1, 1, 3 * d), lambda i, j: (i, 0, 0)),
            pl.BlockSpec((d, d), lambda i, j: (0, 0)),
        ],
        out_specs=pl.BlockSpec((1, tm, d), lambda i, j: (i, j, 0)),
        out_shape=jax.ShapeDtypeStruct((b, s, d), F32),
        compiler_params=pltpu.CompilerParams(
            dimension_semantics=("arbitrary", "arbitrary"), vmem_limit_bytes=VMEM_LIMIT),
        name="post",
    )(y, u, x, g_post.reshape(1, d), ada.reshape(b, 1, 3 * d), w_out)


def _sb_kernel(q_ref, k_ref, v_ref, o_ref, vt_ref, acc_ref, *, t, n_sub):
    n_tiles = k_ref.shape[1] // t

    @pl.when(pl.program_id(2) == 0)
    def _():
        eye = jnp.where(lax.broadcasted_iota(jnp.int32, (LANES, LANES), 0)
                        == lax.broadcasted_iota(jnp.int32, (LANES, LANES), 1), 1.0, 0.0).astype(BF16)

        def fill(n, carry):
            rows = pl.ds(pl.multiple_of(n * t, t), t)
            vt = lax.dot_general(eye, v_ref[0, rows, :], _NT, preferred_element_type=F32)
            vt_ref[n] = vt.astype(BF16)
            return carry

        lax.fori_loop(0, n_tiles, fill, 0)

    lane = lax.broadcasted_iota(jnp.int32, (t, LANES), 1)
    kpos = lax.broadcasted_iota(jnp.int32, (t, 2 * t), 0)
    qpos = lax.broadcasted_iota(jnp.int32, (t, 2 * t), 1) % t
    strict = kpos < qpos
    upper = jnp.where(lax.broadcasted_iota(jnp.int32, (t, t), 1)
                      > lax.broadcasted_iota(jnp.int32, (t, t), 0), 1.0, 0.0).astype(BF16)

    def logits(qs, j):
        rows = pl.ds(pl.multiple_of(j * t, t), t)
        return lax.dot_general(k_ref[0, rows, :], qs, _NT, preferred_element_type=F32)

    def soften(z, diag):
        z = jnp.minimum(z, SB_LOGIT_CAP)
        drop = jnp.log2(1.0 + jnp.exp2(z))
        log_beta = z - drop
        return (jnp.where(strict, drop, 0.0) if diag else drop), log_beta

    def suffix(drop):
        return jnp.dot(upper, drop.astype(BF16), preferred_element_type=F32)

    def weigh(j, log_beta, between, carry, diag):
        w = jnp.exp2(log_beta - between - carry)
        if diag:
            w = jnp.where(strict, w, 0.0)
        return jnp.dot(vt_ref[j], w.astype(BF16), preferred_element_type=F32)

    def live(carry):
        return jnp.min(carry) < SB_EXIT_BITS

    work = []
    for sub in range(n_sub):
        qt = pl.program_id(2) * n_sub + sub
        q = q_ref[0, sub * t:(sub + 1) * t, :].astype(F32) * ((SB_HEAD_DIM ** -0.5) * LOG2E)
        qs = jnp.concatenate([jnp.where(lane < SB_HEAD_DIM, q, 0.0).astype(BF16),
                              jnp.where(lane < SB_HEAD_DIM, 0.0, q).astype(BF16)], axis=0)
        j_prev = jnp.maximum(qt - 1, 0)
        work.append(dict(sub=sub, qt=qt, qs=qs, j_prev=j_prev,
                         z_d=logits(qs, qt), z_p=logits(qs, j_prev)))
    for w_ in work:
        w_["drop_d"], w_["lb_d"] = soften(w_["z_d"], True)
        w_["btw_d"] = suffix(w_["drop_d"])
        w_["drop_p"], w_["lb_p"] = soften(w_["z_p"], False)
        w_["btw_p"] = suffix(w_["drop_p"])
    for w_ in work:
        carry = w_["btw_d"][0:1, :] + w_["drop_d"][0:1, :]
        has_prev = jnp.full((1, 2 * t), w_["qt"], jnp.int32) >= 1
        carry = jnp.where(has_prev, carry, -NEG_INF)
        pv = (weigh(w_["qt"], w_["lb_d"], w_["btw_d"], jnp.zeros((1, 2 * t), F32), True)
              + weigh(w_["j_prev"], w_["lb_p"], w_["btw_p"], carry, False))
        acc_ref[w_["sub"]] = pv
        w_["carry"] = carry + w_["btw_p"][0:1, :] + w_["drop_p"][0:1, :]

    for w_ in work:
        def cond(c):
            return jnp.logical_and(c[0] >= 0, c[1])

        def body(c, sub=w_["sub"], qs=w_["qs"]):
            j, _, carry = c
            drop, log_beta = soften(logits(qs, j), False)
            between = suffix(drop)
            acc_ref[sub] += weigh(j, log_beta, between, carry, False)
            carry = carry + between[0:1, :] + drop[0:1, :]
            return j - 1, live(carry), carry

        lax.while_loop(cond, body, (w_["qt"] - 2, live(w_["carry"]), w_["carry"]))
        acc = acc_ref[w_["sub"]]
        out_t = jnp.concatenate([acc[0:SB_HEAD_DIM, 0:t], acc[SB_HEAD_DIM:LANES, t:2 * t]], axis=0)
        o_ref[0, w_["sub"] * t:(w_["sub"] + 1) * t, :] = out_t.T.astype(o_ref.dtype)


def _sb_attention(u, d):
    b, s, _ = u.shape
    t = ATT_TILE
    n_sub = SB_Q_TILES
    npair = d // LANES
    return pl.pallas_call(
        functools.partial(_sb_kernel, t=t, n_sub=n_sub),
        grid=(b, npair, s // (t * n_sub)),
        in_specs=[
            pl.BlockSpec((1, t * n_sub, LANES), lambda i, h, j: (i, j, h)),
            pl.BlockSpec((1, s, LANES), lambda i, h, j: (i, 0, npair + h)),
            pl.BlockSpec((1, s, LANES), lambda i, h, j: (i, 0, 2 * npair + h)),
        ],
        out_specs=pl.BlockSpec((1, t * n_sub, LANES), lambda i, h, j: (i, j, h)),
        out_shape=jax.ShapeDtypeStruct((b, s, d), BF16),
        scratch_shapes=[pltpu.VMEM((s // t, LANES, t), BF16),
                        pltpu.VMEM((n_sub, LANES, 2 * t), F32)],
        compiler_params=pltpu.CompilerParams(
            dimension_semantics=("arbitrary", "arbitrary", "arbitrary"),
            vmem_limit_bytes=VMEM_LIMIT),
        name="sb_attn",
    )(u, u, u)


def _diff_kernel(q_ref, k_ref, v_ref, lam_ref, g_ref, o_ref, vt_ref, kmax_ref, bias_ref, acc_ref,
                 *, t, tq, n_heads, lam_init):
    h = pl.program_id(1)
    qi = pl.program_id(2)
    n_tiles = k_ref.shape[1] // t
    head_no = (lax.broadcasted_iota(jnp.int32, (1, 1), 0) + (h + 1)).astype(F32)
    slope = jnp.exp2(-head_no * (ALIBI_MAX_EXP / n_heads))
    lane = lax.broadcasted_iota(jnp.int32, (tq, LANES), 1)
    n_diag = tq // t
    half = lax.broadcasted_iota(jnp.int32, (SUBLANES, LANES), 0)
    feat = lax.broadcasted_iota(jnp.int32, (SUBLANES, LANES), 1)
    pick = jnp.where((feat >= DIFF_HEAD_DIM) == (half == 1), 1.0, 0.0)
    pick = jnp.where(half < 2, pick, 0.0).astype(BF16)

    def sq_norms(x):
        xf = x.astype(F32)
        return lax.dot_general(pick, (xf * xf).astype(BF16), _NT, preferred_element_type=F32)

    @pl.when(qi == 0)
    def _():
        eye = jnp.where(lax.broadcasted_iota(jnp.int32, (LANES, LANES), 0)
                        == lax.broadcasted_iota(jnp.int32, (LANES, LANES), 1), 1.0, 0.0).astype(BF16)

        def fill(n, kmax):
            rows = pl.ds(pl.multiple_of(n * t, t), t)
            vt = lax.dot_general(eye, v_ref[0, rows, :], _NT, preferred_element_type=F32)
            vt_ref[n, 0:LANES, :] = vt.astype(BF16)
            vt_ref[n, LANES:LANES + ROWS_ONES, :] = jnp.ones((ROWS_ONES, t), BF16)
            kmax = jnp.maximum(kmax, jnp.max(sq_norms(k_ref[0, rows, :]), axis=1, keepdims=True))
            kmax_ref[n] = jnp.broadcast_to(kmax, (SUBLANES, LANES))
            return kmax

        lax.fori_loop(0, n_tiles, fill, jnp.zeros((SUBLANES, 1), F32))
        qpos = lax.broadcasted_iota(jnp.int32, (t, tq), 1)
        for d in range(n_diag):
            kpos = lax.broadcasted_iota(jnp.int32, (t, tq), 0) + d * t
            ahead = jnp.maximum(kpos - qpos, 0).astype(F32)
            bias_ref[d] = jnp.where(kpos // CHUNK <= qpos // CHUNK, -2.0 * slope * ahead, NEG_INF)

    sub = lax.broadcasted_iota(jnp.int32, (t, LANES), 0)
    q_raw = q_ref[0]
    q = q_raw.astype(F32) * (DIFF_HEAD_DIM ** -0.5)
    one_col = jnp.where(lane == 0, 1.0, 0.0).astype(BF16)
    q1 = jnp.concatenate([jnp.where(lane < DIFF_HEAD_DIM, q, 0.0).astype(BF16), one_col], axis=1)
    q2 = jnp.concatenate([jnp.where(lane < DIFF_HEAD_DIM, 0.0, q).astype(BF16), one_col], axis=1)
    k_aug = jnp.where(lax.broadcasted_iota(jnp.int32, (t, LANES), 1) == 0,
                      slope * sub.astype(F32), 0.0).astype(BF16)
    q_pos = lax.broadcasted_iota(jnp.int32, (1, tq), 1) + qi * tq
    q_sq = sq_norms(q_raw) * (BOUND_SLACK / DIFF_HEAD_DIM)

    def key_tile(j):
        k2 = k_ref[0, pl.ds(pl.multiple_of(j * t, t), t), :]
        return jnp.concatenate([k2, k_aug], axis=1)

    def query_bias(j):
        return -slope * (q_pos - j * t).astype(F32)

    def update(idx, scores, biases, vt, m):
        m_new = m
        for s, c in zip(scores, biases):
            m_new = jnp.maximum(m_new, jnp.max(s, axis=0, keepdims=True) + c)
        alpha = jnp.exp(m - m_new)
        p = jnp.concatenate([jnp.exp(s - (m_new - c)).astype(BF16) for s, c in zip(scores, biases)], axis=0)
        acc_ref[idx] = alpha * acc_ref[idx] + jnp.dot(vt, p, preferred_element_type=F32)
        return m_new

    def accumulate(idx, scores, biases, vt, m):
        p = jnp.concatenate([jnp.exp(s - (m - c)).astype(BF16) for s, c in zip(scores, biases)], axis=0)
        acc_ref[idx] += jnp.dot(vt, p, preferred_element_type=F32)
        return m

    def group(j_top, m1, m2, diagonal, step):
        js = [jnp.maximum(j_top - g, 0) for g in range(DIFF_GROUP)]
        cs = [jnp.where(jnp.full((1, tq), j_top - g, jnp.int32) >= 0, query_bias(js[g]), NEG_INF)
              for g in range(DIFF_GROUP)]
        ks = [key_tile(j) for j in js]
        vt = jnp.concatenate([vt_ref[j] for j in js], axis=1)
        s1 = [lax.dot_general(k, q1, _NT, preferred_element_type=F32) for k in ks]
        s2 = [lax.dot_general(k, q2, _NT, preferred_element_type=F32) for k in ks]
        if diagonal:
            for g in range(n_diag):
                s1[g] = s1[g] + bias_ref[n_diag - 1 - g]
                s2[g] = s2[g] + bias_ref[n_diag - 1 - g]
        return step(0, s1, cs, vt, m1), step(1, s2, cs, vt, m2)

    def headroom(j_top, m1, m2):
        k_sq = kmax_ref[jnp.maximum(j_top, 0)]
        reach = jnp.sqrt(q_sq * k_sq[:, 0:1])
        nearest = -slope * (q_pos - ((j_top + 1) * t - 1)).astype(F32)
        return jnp.max(jnp.maximum(reach[0:1] - m1, reach[1:2] - m2) + nearest)

    acc_ref[...] = jnp.zeros(acc_ref.shape, F32)
    m0 = jnp.full((1, tq), NEG_INF, F32)
    j_diag = (qi + 1) * n_diag - 1
    m1, m2 = group(j_diag, m0, m0, True, update)

    def cond(c):
        return jnp.logical_and(c[0] >= 0, c[1] > -EXP_ZERO_BELOW)

    def body(c):
        j_top, room, m1, m2 = c
        j_next = j_top - DIFF_GROUP

        def fast():
            return (headroom(j_next, m1, m2),) + group(j_top, m1, m2, False, accumulate)

        def exact():
            return (headroom(j_next, m1, m2),) + group(j_top, m1, m2, False, update)

        room, m1, m2 = lax.cond(room < FAST_BELOW, fast, exact)
        return j_next, room, m1, m2

    j_first = j_diag - DIFF_GROUP
    lax.while_loop(cond, body, (j_first, headroom(j_first, m1, m2), m1, m2))

    lam = lam_ref[...]
    lam_full = (jnp.exp(jnp.sum(lam[0:1] * lam[1:2], keepdims=True))
                - jnp.exp(jnp.sum(lam[2:3] * lam[3:4], keepdims=True)) + lam_init)
    a1 = acc_ref[0]
    a2 = acc_ref[1]
    out_t = (a1[0:LANES] / a1[LANES:LANES + 1]
             - lam_full * (a2[0:LANES] / a2[LANES:LANES + 1]))
    out = out_t.T
    ms = jnp.mean(out * out, axis=-1, keepdims=True)
    out = out * lax.rsqrt(ms + RMS_EPS) * g_ref[...] * (1.0 - lam_init)
    o_ref[0] = out.astype(o_ref.dtype)


def _diff_attention(u, d, lam, g_sub, lam_init):
    b, s, _ = u.shape
    t = ATT_TILE
    tq = DIFF_Q_TILE
    n_heads = d // LANES
    return pl.pallas_call(
        functools.partial(_diff_kernel, t=t, tq=tq, n_heads=n_heads, lam_init=lam_init),
        grid=(b, n_heads, s // tq),
        in_specs=[
            pl.BlockSpec((1, tq, LANES), lambda i, h, j: (i, j, h)),
            pl.BlockSpec((1, s, LANES), lambda i, h, j: (i, 0, n_heads + h)),
            pl.BlockSpec((1, s, LANES), lambda i, h, j: (i, 0, 2 * n_heads + h)),
            pl.BlockSpec(lam.shape, lambda i, h, j: (0, 0)),
            pl.BlockSpec((1, LANES), lambda i, h, j: (0, 0)),
        ],
        out_specs=pl.BlockSpec((1, tq, LANES), lambda i, h, j: (i, j, h)),
        out_shape=jax.ShapeDtypeStruct((b, s, d), BF16),
        scratch_shapes=[pltpu.VMEM((s // t, LANES + ROWS_ONES, t), BF16),
                        pltpu.VMEM((s // t, SUBLANES, LANES), F32),
                        pltpu.VMEM((tq // t, t, tq), F32),
                        pltpu.VMEM((2, LANES + ROWS_ONES, tq), F32)],
        compiler_params=pltpu.CompilerParams(
            dimension_semantics=("arbitrary", "arbitrary", "arbitrary"),
            vmem_limit_bytes=VMEM_LIMIT),
        name="diff_attn",
    )(u, u, u, lam, g_sub.reshape(1, LANES))


def kernel(x, c, w_ada, b_ada, g_pre, g_post, w_in, w_out, diff_lambda, diff_subln):
    depth = w_ada.shape[0]
    d = x.shape[-1]
    assert w_in.shape[2] == 4 * d and d % LANES == 0
    assert x.shape[1] % ROW_TILE == 0 and x.shape[1] % DIFF_Q_TILE == 0
    assert DIFF_Q_TILE % ATT_TILE == 0 and ATT_TILE % CHUNK == 0
    ada = _ada(c, w_ada, b_ada)
    w_in_b = w_in.astype(BF16)
    w_out_b = w_out.astype(BF16)
    for i in range(depth):
        u = _pre(x, g_pre[i], ada[i], w_in_b[i])
        if i % N_MIXERS == 0:
            y = _sb_attention(u, d)
        else:
            j = i // N_MIXERS
            y = _diff_attention(u, d, diff_lambda[j], diff_subln[j], _diff_lambda_init(i))
        x = _post(y, u, x, g_post[i], ada[i], w_out_b[i])
    return x
```

```python
import functools
import math

import jax
import jax.numpy as jnp
from jax import lax
from jax.experimental import pallas as pl
from jax.experimental.pallas import tpu as pltpu

F32 = jnp.float32
BF16 = jnp.bfloat16

LANES = 128
SUBLANES = 8
N_MIXERS = 2
CHUNK = 64
SB_HEAD_DIM = 64
DIFF_HEAD_DIM = 64
ALIBI_MAX_EXP = 8.0
RMS_EPS = 1e-6
NEG_INF = -1e30
LOG2E = 1.4426950408889634
SB_LOGIT_CAP = 126.0
SB_EXIT_BITS = 160.0
DIFF_GROUP = 4
EXP_ZERO_BELOW = 104.0
FAST_BELOW = 60.0
BOUND_SLACK = 1.05
ROWS_ONES = 16

ROW_TILE = 512
ATT_TILE = 256
SB_Q_TILES = 2
DIFF_Q_TILE = 512
VMEM_LIMIT = 48 * 1024 * 1024

_NT = (((1,), (1,)), ((), ()))


def _diff_lambda_init(layer_idx):
    return 0.8 - 0.6 * math.exp(-0.3 * layer_idx)


def _ada_kernel(c_ref, w_ref, b_ref, o_ref):
    c = c_ref[...]
    cond = c * jax.nn.sigmoid(c)
    o_ref[0] = jnp.dot(cond, w_ref[0], preferred_element_type=F32,
                       precision=lax.Precision.HIGHEST) + b_ref[0]


def _ada(c, w_ada, b_ada):
    depth, d, d3 = w_ada.shape
    nb = c.shape[0]
    b = -(-nb // SUBLANES) * SUBLANES
    c = jnp.pad(c, ((0, b - nb), (0, 0)))
    nt = d3 // d
    out = pl.pallas_call(
        _ada_kernel,
        grid=(depth, nt),
        in_specs=[
            pl.BlockSpec((b, d), lambda i, j: (0, 0)),
            pl.BlockSpec((1, d, d), lambda i, j: (i, 0, j)),
            pl.BlockSpec((1, 1, d), lambda i, j: (i, 0, j)),
        ],
        out_specs=pl.BlockSpec((1, b, d), lambda i, j: (i, 0, j)),
        out_shape=jax.ShapeDtypeStruct((depth, b, d3), F32),
        name="ada",
    )(c, w_ada, b_ada.reshape(depth, 1, d3))
    return out[:, :nb]


def _pre_kernel(x_ref, g_ref, ada_ref, w_ref, u_ref, *, d):
    x = x_ref[0]
    ms = jnp.mean(x * x, axis=-1, keepdims=True)
    y = x * lax.rsqrt(ms + RMS_EPS) * g_ref[...]
    ada = ada_ref[0]
    h = (y * (1.0 + ada[:, d:2 * d]) + ada[:, 0:d]).astype(BF16)
    n_out = w_ref.shape[1]
    for n in range(n_out // d):
        u_ref[0, :, n * d:(n + 1) * d] = jnp.dot(
            h, w_ref[:, n * d:(n + 1) * d], preferred_element_type=F32).astype(BF16)


def _pre(x, g_pre, ada, w_in):
    b, s, d = x.shape
    n_out = w_in.shape[1]
    tm = ROW_TILE
    return pl.pallas_call(
        functools.partial(_pre_kernel, d=d),
        grid=(b, s // tm),
        in_specs=[
            pl.BlockSpec((1, tm, d), lambda i, j: (i, j, 0)),
            pl.BlockSpec((1, d), lambda i, j: (0, 0)),
            pl.BlockSpec((1, 1, 3 * d), lambda i, j: (i, 0, 0)),
            pl.BlockSpec((d, n_out), lambda i, j: (0, 0)),
        ],
        out_specs=pl.BlockSpec((1, tm, n_out), lambda i, j: (i, j, 0)),
        out_shape=jax.ShapeDtypeStruct((b, s, n_out), BF16),
        compiler_params=pltpu.CompilerParams(
            dimension_semantics=("arbitrary", "arbitrary"), vmem_limit_bytes=VMEM_LIMIT),
        name="pre",
    )(x, g_pre.reshape(1, d), ada.reshape(b, 1, 3 * d), w_in)


def _post_kernel(y_ref, z_ref, x_ref, g_ref, ada_ref, w_ref, o_ref, *, d):
    y = y_ref[0].astype(F32)
    z = z_ref[0].astype(F32)
    t = (y * (z * jax.nn.sigmoid(z))).astype(BF16)
    r = jnp.dot(t, w_ref[...], preferred_element_type=F32)
    ms = jnp.mean(r * r, axis=-1, keepdims=True)
    rn = r * lax.rsqrt(ms + RMS_EPS) * g_ref[...]
    gate = ada_ref[0][:, 2 * d:3 * d]
    o_ref[0] = x_ref[0] + gate * rn


def _post(y, u, x, g_post, ada, w_out):
    b, s, d = x.shape
    tm = ROW_TILE
    z_col = u.shape[2] // d - 1
    return pl.pallas_call(
        functools.partial(_post_kernel, d=d),
        grid=(b, s // tm),
        in_specs=[
            pl.BlockSpec((1, tm, d), lambda i, j: (i, j, 0)),
            pl.BlockSpec((1, tm, d), lambda i, j: (i, j, z_col)),
            pl.BlockSpec((1, tm, d), lambda i, j: (i, j, 0)),
            pl.BlockSpec((1, d), lambda i, j: (0, 0)),
            pl.BlockSpec((1, 1, 3 * d), lambda i, j: (i, 0, 0)),
            pl.BlockSpec((d, d), lambda i, j: (0, 0)),
        ],
        out_specs=pl.BlockSpec((1, tm, d), lambda i, j: (i, j, 0)),
        out_shape=jax.ShapeDtypeStruct((b, s, d), F32),
        compiler_params=pltpu.CompilerParams(
            dimension_semantics=("arbitrary", "arbitrary"), vmem_limit_bytes=VMEM_LIMIT),
        name="post",
    )(y, u, x, g_post.reshape(1, d), ada.reshape(b, 1, 3 * d), w_out)


def _sb_kernel(q_ref, k_ref, v_ref, o_ref, vt_ref, acc_ref, *, t, n_sub):
    eye = jnp.where(lax.broadcasted_iota(jnp.int32, (LANES, LANES), 0)
                    == lax.broadcasted_iota(jnp.int32, (LANES, LANES), 1), 1.0, 0.0).astype(BF16)
    for sub in range(n_sub):
        j = pl.program_id(2) * n_sub + sub
        v_tile = v_ref[0, pl.ds(pl.multiple_of(j * t, t), t), :]
        vt_ref[j] = lax.dot_general(eye, v_tile, _NT, preferred_element_type=F32).astype(BF16)

    lane = lax.broadcasted_iota(jnp.int32, (t, LANES), 1)
    kpos = lax.broadcasted_iota(jnp.int32, (t, 2 * t), 0)
    qpos = lax.broadcasted_iota(jnp.int32, (t, 2 * t), 1) % t
    strict = kpos < qpos
    upper = jnp.where(lax.broadcasted_iota(jnp.int32, (t, t), 1)
                      > lax.broadcasted_iota(jnp.int32, (t, t), 0), 1.0, 0.0).astype(BF16)

    def logits(qs, j):
        rows = pl.ds(pl.multiple_of(j * t, t), t)
        return lax.dot_general(k_ref[0, rows, :], qs, _NT, preferred_element_type=F32)

    def soften(z, diag):
        z = jnp.minimum(z, SB_LOGIT_CAP)
        drop = jnp.log2(1.0 + jnp.exp2(z))
        log_beta = z - drop
        return (jnp.where(strict, drop, 0.0) if diag else drop), log_beta

    def suffix(drop):
        return jnp.dot(upper, drop.astype(BF16), preferred_element_type=F32)

    def weigh(j, log_beta, between, carry, diag):
        w = jnp.exp2(log_beta - between - carry)
        if diag:
            w = jnp.where(strict, w, 0.0)
        return jnp.dot(vt_ref[j], w.astype(BF16), preferred_element_type=F32)

    def live(carry):
        return jnp.min(carry) < SB_EXIT_BITS

    work = []
    for sub in range(n_sub):
        qt = pl.program_id(2) * n_sub + sub
        q = q_ref[0, sub * t:(sub + 1) * t, :].astype(F32) * ((SB_HEAD_DIM ** -0.5) * LOG2E)
        qs = jnp.concatenate([jnp.where(lane < SB_HEAD_DIM, q, 0.0).astype(BF16),
                              jnp.where(lane < SB_HEAD_DIM, 0.0, q).astype(BF16)], axis=0)
        j_prev = jnp.maximum(qt - 1, 0)
        work.append(dict(sub=sub, qt=qt, qs=qs, j_prev=j_prev,
                         z_d=logits(qs, qt), z_p=logits(qs, j_prev)))
    for w_ in work:
        w_["drop_d"], w_["lb_d"] = soften(w_["z_d"], True)
        w_["btw_d"] = suffix(w_["drop_d"])
        w_["drop_p"], w_["lb_p"] = soften(w_["z_p"], False)
        w_["btw_p"] = suffix(w_["drop_p"])
    for w_ in work:
        carry = w_["btw_d"][0:1, :] + w_["drop_d"][0:1, :]
        has_prev = jnp.full((1, 2 * t), w_["qt"], jnp.int32) >= 1
        carry = jnp.where(has_prev, carry, -NEG_INF)
        pv = (weigh(w_["qt"], w_["lb_d"], w_["btw_d"], jnp.zeros((1, 2 * t), F32), True)
              + weigh(w_["j_prev"], w_["lb_p"], w_["btw_p"], carry, False))
        acc_ref[w_["sub"]] = pv
        w_["carry"] = carry + w_["btw_p"][0:1, :] + w_["drop_p"][0:1, :]

    for w_ in work:
        def cond(c):
            return jnp.logical_and(c[0] >= 0, c[1])

        def body(c, sub=w_["sub"], qs=w_["qs"]):
            j, _, carry = c
            drop, log_beta = soften(logits(qs, j), False)
            between = suffix(drop)
            acc_ref[sub] += weigh(j, log_beta, between, carry, False)
            carry = carry + between[0:1, :] + drop[0:1, :]
            return j - 1, live(carry), carry

        lax.while_loop(cond, body, (w_["qt"] - 2, live(w_["carry"]), w_["carry"]))
        acc = acc_ref[w_["sub"]]
        out_t = jnp.concatenate([acc[0:SB_HEAD_DIM, 0:t], acc[SB_HEAD_DIM:LANES, t:2 * t]], axis=0)
        o_ref[0, w_["sub"] * t:(w_["sub"] + 1) * t, :] = out_t.T.astype(o_ref.dtype)


def _sb_attention(u, d):
    b, s, _ = u.shape
    t = ATT_TILE
    n_sub = SB_Q_TILES
    npair = d // LANES
    return pl.pallas_call(
        functools.partial(_sb_kernel, t=t, n_sub=n_sub),
        grid=(b, npair, s // (t * n_sub)),
        in_specs=[
            pl.BlockSpec((1, t * n_sub, LANES), lambda i, h, j: (i, j, h)),
            pl.BlockSpec((1, s, LANES), lambda i, h, j: (i, 0, npair + h)),
            pl.BlockSpec((1, s, LANES), lambda i, h, j: (i, 0, 2 * npair + h)),
        ],
        out_specs=pl.BlockSpec((1, t * n_sub, LANES), lambda i, h, j: (i, j, h)),
        out_shape=jax.ShapeDtypeStruct((b, s, d), BF16),
        scratch_shapes=[pltpu.VMEM((s // t, LANES, t), BF16),
                        pltpu.VMEM((n_sub, LANES, 2 * t), F32)],
        compiler_params=pltpu.CompilerParams(
            dimension_semantics=("arbitrary", "arbitrary", "arbitrary"),
            vmem_limit_bytes=VMEM_LIMIT),
        name="sb_attn",
    )(u, u, u)


def _diff_kernel(q_ref, k_ref, v_ref, lam_ref, g_ref, o_ref, vt_ref, kmax_ref, bias_ref, acc_ref,
                 *, t, tq, n_heads, lam_init):
    h = pl.program_id(1)
    qi = pl.program_id(2)
    head_no = (lax.broadcasted_iota(jnp.int32, (1, 1), 0) + (h + 1)).astype(F32)
    slope = jnp.exp2(-head_no * (ALIBI_MAX_EXP / n_heads))
    lane = lax.broadcasted_iota(jnp.int32, (tq, LANES), 1)
    n_diag = tq // t
    half = lax.broadcasted_iota(jnp.int32, (SUBLANES, LANES), 0)
    feat = lax.broadcasted_iota(jnp.int32, (SUBLANES, LANES), 1)
    pick = jnp.where((feat >= DIFF_HEAD_DIM) == (half == 1), 1.0, 0.0)
    pick = jnp.where(half < 2, pick, 0.0).astype(BF16)

    def sq_norms(x):
        xf = x.astype(F32)
        return lax.dot_general(pick, (xf * xf).astype(BF16), _NT, preferred_element_type=F32)

    @pl.when(qi == 0)
    def _():
        kmax_ref[0] = jnp.zeros((SUBLANES, LANES), F32)
        qpos = lax.broadcasted_iota(jnp.int32, (t, tq), 1)
        for d in range(n_diag):
            kpos = lax.broadcasted_iota(jnp.int32, (t, tq), 0) + d * t
            ahead = jnp.maximum(kpos - qpos, 0).astype(F32)
            bias_ref[d] = jnp.where(kpos // CHUNK <= qpos // CHUNK, -2.0 * slope * ahead, NEG_INF)

    eye = jnp.where(lax.broadcasted_iota(jnp.int32, (LANES, LANES), 0)
                    == lax.broadcasted_iota(jnp.int32, (LANES, LANES), 1), 1.0, 0.0).astype(BF16)
    for d in range(n_diag):
        j = qi * n_diag + d
        rows = pl.ds(pl.multiple_of(j * t, t), t)
        vt = lax.dot_general(eye, v_ref[0, rows, :], _NT, preferred_element_type=F32)
        vt_ref[j, 0:LANES, :] = vt.astype(BF16)
        vt_ref[j, LANES:LANES + ROWS_ONES, :] = jnp.ones((ROWS_ONES, t), BF16)
        k_sq = jnp.max(sq_norms(k_ref[0, rows, :]), axis=1, keepdims=True)
        kmax_ref[j + 1] = jnp.maximum(kmax_ref[j], k_sq)

    sub = lax.broadcasted_iota(jnp.int32, (t, LANES), 0)
    q_raw = q_ref[0]
    q = q_raw.astype(F32) * (DIFF_HEAD_DIM ** -0.5)
    one_col = jnp.where(lane == 0, 1.0, 0.0).astype(BF16)
    q1 = jnp.concatenate([jnp.where(lane < DIFF_HEAD_DIM, q, 0.0).astype(BF16), one_col], axis=1)
    q2 = jnp.concatenate([jnp.where(lane < DIFF_HEAD_DIM, 0.0, q).astype(BF16), one_col], axis=1)
    k_aug = jnp.where(lax.broadcasted_iota(jnp.int32, (t, LANES), 1) == 0,
                      slope * sub.astype(F32), 0.0).astype(BF16)
    q_pos = lax.broadcasted_iota(jnp.int32, (1, tq), 1) + qi * tq
    q_sq = sq_norms(q_raw) * (BOUND_SLACK / DIFF_HEAD_DIM)

    def key_tile(j):
        k2 = k_ref[0, pl.ds(pl.multiple_of(j * t, t), t), :]
        return jnp.concatenate([k2, k_aug], axis=1)

    def query_bias(j):
        return -slope * (q_pos - j * t).astype(F32)

    def update(idx, scores, biases, vt, m):
        m_new = m
        for s, c in zip(scores, biases):
            m_new = jnp.maximum(m_new, jnp.max(s, axis=0, keepdims=True) + c)
        alpha = jnp.exp(m - m_new)
        p = jnp.concatenate([jnp.exp(s - (m_new - c)).astype(BF16) for s, c in zip(scores, biases)], axis=0)
        acc_ref[idx] = alpha * acc_ref[idx] + jnp.dot(vt, p, preferred_element_type=F32)
        return m_new

    def accumulate(idx, scores, biases, vt, m):
        p = jnp.concatenate([jnp.exp(s - (m - c)).astype(BF16) for s, c in zip(scores, biases)], axis=0)
        acc_ref[idx] += jnp.dot(vt, p, preferred_element_type=F32)
        return m

    def group(j_top, m1, m2, diagonal, step):
        js = [jnp.maximum(j_top - g, 0) for g in range(DIFF_GROUP)]
        cs = [jnp.where(jnp.full((1, tq), j_top - g, jnp.int32) >= 0, query_bias(js[g]), NEG_INF)
              for g in range(DIFF_GROUP)]
        ks = [key_tile(j) for j in js]
        vt = jnp.concatenate([vt_ref[j] for j in js], axis=1)
        s1 = [lax.dot_general(k, q1, _NT, preferred_element_type=F32) for k in ks]
        s2 = [lax.dot_general(k, q2, _NT, preferred_element_type=F32) for k in ks]
        if diagonal:
            for g in range(n_diag):
                s1[g] = s1[g] + bias_ref[n_diag - 1 - g]
                s2[g] = s2[g] + bias_ref[n_diag - 1 - g]
        return step(0, s1, cs, vt, m1), step(1, s2, cs, vt, m2)

    def headroom(j_top, m1, m2):
        k_sq = kmax_ref[jnp.maximum(j_top, 0) + 1]
        reach = jnp.sqrt(q_sq * k_sq[:, 0:1])
        nearest = -slope * (q_pos - ((j_top + 1) * t - 1)).astype(F32)
        return jnp.max(jnp.maximum(reach[0:1] - m1, reach[1:2] - m2) + nearest)

    acc_ref[...] = jnp.zeros(acc_ref.shape, F32)
    m0 = jnp.full((1, tq), NEG_INF, F32)
    j_diag = (qi + 1) * n_diag - 1
    m1, m2 = group(j_diag, m0, m0, True, update)

    def cond(c):
        return jnp.logical_and(c[0] >= 0, c[1] > -EXP_ZERO_BELOW)

    def body(c):
        j_top, room, m1, m2 = c
        j_next = j_top - DIFF_GROUP

        def fast():
            return (headroom(j_next, m1, m2),) + group(j_top, m1, m2, False, accumulate)

        def exact():
            return (headroom(j_next, m1, m2),) + group(j_top, m1, m2, False, update)

        room, m1, m2 = lax.cond(room < FAST_BELOW, fast, exact)
        return j_next, room, m1, m2

    j_first = j_diag - DIFF_GROUP
    lax.while_loop(cond, body, (j_first, headroom(j_first, m1, m2), m1, m2))

    lam = lam_ref[...]
    lam_full = (jnp.exp(jnp.sum(lam[0:1] * lam[1:2], keepdims=True))
                - jnp.exp(jnp.sum(lam[2:3] * lam[3:4], keepdims=True)) + lam_init)
    a1 = acc_ref[0]
    a2 = acc_ref[1]
    out_t = (a1[0:LANES] / a1[LANES:LANES + 1]
             - lam_full * (a2[0:LANES] / a2[LANES:LANES + 1]))
    out = out_t.T
    ms = jnp.mean(out * out, axis=-1, keepdims=True)
    out = out * lax.rsqrt(ms + RMS_EPS) * g_ref[...] * (1.0 - lam_init)
    o_ref[0] = out.astype(o_ref.dtype)


def _diff_attention(u, d, lam, g_sub, lam_init):
    b, s, _ = u.shape
    t = ATT_TILE
    tq = DIFF_Q_TILE
    n_heads = d // LANES
    return pl.pallas_call(
        functools.partial(_diff_kernel, t=t, tq=tq, n_heads=n_heads, lam_init=lam_init),
        grid=(b, n_heads, s // tq),
        in_specs=[
            pl.BlockSpec((1, tq, LANES), lambda i, h, j: (i, j, h)),
            pl.BlockSpec((1, s, LANES), lambda i, h, j: (i, 0, n_heads + h)),
            pl.BlockSpec((1, s, LANES), lambda i, h, j: (i, 0, 2 * n_heads + h)),
            pl.BlockSpec(lam.shape, lambda i, h, j: (0, 0)),
            pl.BlockSpec((1, LANES), lambda i, h, j: (0, 0)),
        ],
        out_specs=pl.BlockSpec((1, tq, LANES), lambda i, h, j: (i, j, h)),
        out_shape=jax.ShapeDtypeStruct((b, s, d), BF16),
        scratch_shapes=[pltpu.VMEM((s // t, LANES + ROWS_ONES, t), BF16),
                        pltpu.VMEM((s // t + 1, SUBLANES, LANES), F32),
                        pltpu.VMEM((tq // t, t, tq), F32),
                        pltpu.VMEM((2, LANES + ROWS_ONES, tq), F32)],
        compiler_params=pltpu.CompilerParams(
            dimension_semantics=("arbitrary", "arbitrary", "arbitrary"),
            vmem_limit_bytes=VMEM_LIMIT),
        name="diff_attn",
    )(u, u, u, lam, g_sub.reshape(1, LANES))


def kernel(x, c, w_ada, b_ada, g_pre, g_post, w_in, w_out, diff_lambda, diff_subln):
    depth = w_ada.shape[0]
    d = x.shape[-1]
    assert w_in.shape[2] == 4 * d and d % LANES == 0
    assert x.shape[1] % ROW_TILE == 0 and x.shape[1] % DIFF_Q_TILE == 0
    assert DIFF_Q_TILE % ATT_TILE == 0 and ATT_TILE % CHUNK == 0
    ada = _ada(c, w_ada, b_ada)
    w_in_b = w_in.astype(BF16)
    w_out_b = w_out.astype(BF16)
    for i in range(depth):
        u = _pre(x, g_pre[i], ada[i], w_in_b[i])
        if i % N_MIXERS == 0:
            y = _sb_attention(u, d)
        else:
            j = i // N_MIXERS
            y = _diff_attention(u, d, diff_lambda[j], diff_subln[j], _diff_lambda_init(i))
        x = _post(y, u, x, g_post[i], ada[i], w_out_b[i])
    return x
```

```python
import functools
import math

import jax
import jax.numpy as jnp
from jax import lax
from jax.experimental import pallas as pl
from jax.experimental.pallas import tpu as pltpu

F32 = jnp.float32
BF16 = jnp.bfloat16

LANES = 128
SUBLANES = 8
N_MIXERS = 2
CHUNK = 64
SB_HEAD_DIM = 64
DIFF_HEAD_DIM = 64
ALIBI_MAX_EXP = 8.0
RMS_EPS = 1e-6
NEG_INF = -1e30
LOG2E = 1.4426950408889634
SB_LOGIT_CAP = 126.0
SB_EXIT_BITS = 160.0
DIFF_GROUP = 4
EXP_ZERO_BELOW = 104.0
FAST_BELOW = 60.0
BOUND_SLACK = 1.05
ROWS_ONES = 16

ROW_TILE = 512
ATT_TILE = 256
SB_Q_TILES = 2
DIFF_Q_TILE = 1024
VMEM_LIMIT = 48 * 1024 * 1024

_NT = (((1,), (1,)), ((), ()))


def _diff_lambda_init(layer_idx):
    return 0.8 - 0.6 * math.exp(-0.3 * layer_idx)


def _ada_kernel(c_ref, w_ref, b_ref, o_ref):
    c = c_ref[...]
    cond = c * jax.nn.sigmoid(c)
    o_ref[0] = jnp.dot(cond, w_ref[0], preferred_element_type=F32,
                       precision=lax.Precision.HIGHEST) + b_ref[0]


def _ada(c, w_ada, b_ada):
    depth, d, d3 = w_ada.shape
    nb = c.shape[0]
    b = -(-nb // SUBLANES) * SUBLANES
    c = jnp.pad(c, ((0, b - nb), (0, 0)))
    nt = d3 // d
    out = pl.pallas_call(
        _ada_kernel,
        grid=(depth, nt),
        in_specs=[
            pl.BlockSpec((b, d), lambda i, j: (0, 0)),
            pl.BlockSpec((1, d, d), lambda i, j: (i, 0, j)),
            pl.BlockSpec((1, 1, d), lambda i, j: (i, 0, j)),
        ],
        out_specs=pl.BlockSpec((1, b, d), lambda i, j: (i, 0, j)),
        out_shape=jax.ShapeDtypeStruct((depth, b, d3), F32),
        name="ada",
    )(c, w_ada, b_ada.reshape(depth, 1, d3))
    return out[:, :nb]


def _pre_kernel(x_ref, g_ref, ada_ref, w_ref, u_ref, *, d):
    x = x_ref[0]
    ms = jnp.mean(x * x, axis=-1, keepdims=True)
    y = x * lax.rsqrt(ms + RMS_EPS) * g_ref[...]
    ada = ada_ref[0]
    h = (y * (1.0 + ada[:, d:2 * d]) + ada[:, 0:d]).astype(BF16)
    n_out = w_ref.shape[1]
    for n in range(n_out // d):
        u_ref[0, :, n * d:(n + 1) * d] = jnp.dot(
            h, w_ref[:, n * d:(n + 1) * d], preferred_element_type=F32).astype(BF16)


def _pre(x, g_pre, ada, w_in):
    b, s, d = x.shape
    n_out = w_in.shape[1]
    tm = ROW_TILE
    return pl.pallas_call(
        functools.partial(_pre_kernel, d=d),
        grid=(b, s // tm),
        in_specs=[
            pl.BlockSpec((1, tm, d), lambda i, j: (i, j, 0)),
            pl.BlockSpec((1, d), lambda i, j: (0, 0)),
            pl.BlockSpec((1, 1, 3 * d), lambda i, j: (i, 0, 0)),
            pl.BlockSpec((d, n_out), lambda i, j: (0, 0)),
        ],
        out_specs=pl.BlockSpec((1, tm, n_out), lambda i, j: (i, j, 0)),
        out_shape=jax.ShapeDtypeStruct((b, s, n_out), BF16),
        compiler_params=pltpu.CompilerParams(
            dimension_semantics=("arbitrary", "arbitrary"), vmem_limit_bytes=VMEM_LIMIT),
        name="pre",
    )(x, g_pre.reshape(1, d), ada.reshape(b, 1, 3 * d), w_in)


def _post_kernel(y_ref, z_ref, x_ref, g_ref, ada_ref, w_ref, o_ref, *, d):
    y = y_ref[0].astype(F32)
    z = z_ref[0].astype(F32)
    t = (y * (z * jax.nn.sigmoid(z))).astype(BF16)
    r = jnp.dot(t, w_ref[...], preferred_element_type=F32)
    ms = jnp.mean(r * r, axis=-1, keepdims=True)
    rn = r * lax.rsqrt(ms + RMS_EPS) * g_ref[...]
    gate = ada_ref[0][:, 2 * d:3 * d]
    o_ref[0] = x_ref[0] + gate * rn


def _post(y, u, x, g_post, ada, w_out):
    b, s, d = x.shape
    tm = ROW_TILE
    z_col = u.shape[2] // d - 1
    return pl.pallas_call(
        functools.partial(_post_kernel, d=d),
        grid=(b, s // tm),
        in_specs=[
            pl.BlockSpec((1, tm, d), lambda i, j: (i, j, 0)),
            pl.BlockSpec((1, tm, d), lambda i, j: (i, j, z_col)),
            pl.BlockSpec((1, tm, d), lambda i, j: (i, j, 0)),
            pl.BlockSpec((1, d), lambda i, j: (0, 0)),
            pl.BlockSpec((1, 1, 3 * d), lambda i, j: (i, 0, 0)),
            pl.BlockSpec((d, d), lambda i, j: (0, 0)),
        ],
        out_specs=pl.BlockSpec((1, tm, d), lambda i, j: (i, j, 0)),
        out_shape=jax.ShapeDtypeStruct((b, s, d), F32),
        compiler_params=pltpu.CompilerParams(
            dimension_semantics=("arbitrary", "arbitrary"), vmem_limit_bytes=VMEM_LIMIT),
        name="post",
    )(y, u, x, g_post.reshape(1, d), ada.reshape(b, 1, 3 * d), w_out)


def _sb_kernel(q_ref, k_ref, v_ref, o_ref, vt_ref, acc_ref, *, t, n_sub):
    eye = jnp.where(lax.broadcasted_iota(jnp.int32, (LANES, LANES), 0)
                    == lax.broadcasted_iota(jnp.int32, (LANES, LANES), 1), 1.0, 0.0).astype(BF16)
    for sub in range(n_sub):
        j = pl.program_id(2) * n_sub + sub
        v_tile = v_ref[0, pl.ds(pl.multiple_of(j * t, t), t), :]
        vt_ref[j] = lax.dot_general(eye, v_tile, _NT, preferred_element_type=F32).astype(BF16)

    lane = lax.broadcasted_iota(jnp.int32, (t, LANES), 1)
    kpos = lax.broadcasted_iota(jnp.int32, (t, 2 * t), 0)
    qpos = lax.broadcasted_iota(jnp.int32, (t, 2 * t), 1) % t
    strict = kpos < qpos
    upper = jnp.where(lax.broadcasted_iota(jnp.int32, (t, t), 1)
                      > lax.broadcasted_iota(jnp.int32, (t, t), 0), 1.0, 0.0).astype(BF16)

    def logits(qs, j):
        rows = pl.ds(pl.multiple_of(j * t, t), t)
        return lax.dot_general(k_ref[0, rows, :], qs, _NT, preferred_element_type=F32)

    def soften(z, diag):
        z = jnp.minimum(z, SB_LOGIT_CAP)
        drop = jnp.log2(1.0 + jnp.exp2(z))
        log_beta = z - drop
        return (jnp.where(strict, drop, 0.0) if diag else drop), log_beta

    def suffix(drop):
        return jnp.dot(upper, drop.astype(BF16), preferred_element_type=F32)

    def weigh(j, log_beta, between, carry, diag):
        w = jnp.exp2(log_beta - between - carry)
        if diag:
            w = jnp.where(strict, w, 0.0)
        return jnp.dot(vt_ref[j], w.astype(BF16), preferred_element_type=F32)

    def live(carry):
        return jnp.min(carry) < SB_EXIT_BITS

    work = []
    for sub in range(n_sub):
        qt = pl.program_id(2) * n_sub + sub
        q = q_ref[0, sub * t:(sub + 1) * t, :].astype(F32) * ((SB_HEAD_DIM ** -0.5) * LOG2E)
        qs = jnp.concatenate([jnp.where(lane < SB_HEAD_DIM, q, 0.0).astype(BF16),
                              jnp.where(lane < SB_HEAD_DIM, 0.0, q).astype(BF16)], axis=0)
        j_prev = jnp.maximum(qt - 1, 0)
        work.append(dict(sub=sub, qt=qt, qs=qs, j_prev=j_prev,
                         z_d=logits(qs, qt), z_p=logits(qs, j_prev)))
    for w_ in work:
        w_["drop_d"], w_["lb_d"] = soften(w_["z_d"], True)
        w_["btw_d"] = suffix(w_["drop_d"])
        w_["drop_p"], w_["lb_p"] = soften(w_["z_p"], False)
        w_["btw_p"] = suffix(w_["drop_p"])
    for w_ in work:
        carry = w_["btw_d"][0:1, :] + w_["drop_d"][0:1, :]
        has_prev = jnp.full((1, 2 * t), w_["qt"], jnp.int32) >= 1
        carry = jnp.where(has_prev, carry, -NEG_INF)
        pv = (weigh(w_["qt"], w_["lb_d"], w_["btw_d"], jnp.zeros((1, 2 * t), F32), True)
              + weigh(w_["j_prev"], w_["lb_p"], w_["btw_p"], carry, False))
        acc_ref[w_["sub"]] = pv
        w_["carry"] = carry + w_["btw_p"][0:1, :] + w_["drop_p"][0:1, :]

    for w_ in work:
        def cond(c):
            return jnp.logical_and(c[0] >= 0, c[1])

        def body(c, sub=w_["sub"], qs=w_["qs"]):
            j, _, carry = c
            drop, log_beta = soften(logits(qs, j), False)
            between = suffix(drop)
            acc_ref[sub] += weigh(j, log_beta, between, carry, False)
            carry = carry + between[0:1, :] + drop[0:1, :]
            return j - 1, live(carry), carry

        lax.while_loop(cond, body, (w_["qt"] - 2, live(w_["carry"]), w_["carry"]))
        acc = acc_ref[w_["sub"]]
        out_t = jnp.concatenate([acc[0:SB_HEAD_DIM, 0:t], acc[SB_HEAD_DIM:LANES, t:2 * t]], axis=0)
        o_ref[0, w_["sub"] * t:(w_["sub"] + 1) * t, :] = out_t.T.astype(o_ref.dtype)


def _sb_attention(u, d):
    b, s, _ = u.shape
    t = ATT_TILE
    n_sub = SB_Q_TILES
    npair = d // LANES
    return pl.pallas_call(
        functools.partial(_sb_kernel, t=t, n_sub=n_sub),
        grid=(b, npair, s // (t * n_sub)),
        in_specs=[
            pl.BlockSpec((1, t * n_sub, LANES), lambda i, h, j: (i, j, h)),
            pl.BlockSpec((1, s, LANES), lambda i, h, j: (i, 0, npair + h)),
            pl.BlockSpec((1, s, LANES), lambda i, h, j: (i, 0, 2 * npair + h)),
        ],
        out_specs=pl.BlockSpec((1, t * n_sub, LANES), lambda i, h, j: (i, j, h)),
        out_shape=jax.ShapeDtypeStruct((b, s, d), BF16),
        scratch_shapes=[pltpu.VMEM((s // t, LANES, t), BF16),
                        pltpu.VMEM((n_sub, LANES, 2 * t), F32)],
        compiler_params=pltpu.CompilerParams(
            dimension_semantics=("arbitrary", "arbitrary", "arbitrary"),
            vmem_limit_bytes=VMEM_LIMIT),
        name="sb_attn",
    )(u, u, u)


def _diff_kernel(q_ref, k_ref, v_ref, lam_ref, g_ref, o_ref, vt_ref, kmax_ref, bias_ref, acc_ref,
                 *, t, tq, n_heads, lam_init):
    h = pl.program_id(1)
    qi = pl.program_id(2)
    head_no = (lax.broadcasted_iota(jnp.int32, (1, 1), 0) + (h + 1)).astype(F32)
    slope = jnp.exp2(-head_no * (ALIBI_MAX_EXP / n_heads))
    lane = lax.broadcasted_iota(jnp.int32, (tq, LANES), 1)
    n_diag = tq // t
    half = lax.broadcasted_iota(jnp.int32, (SUBLANES, LANES), 0)
    feat = lax.broadcasted_iota(jnp.int32, (SUBLANES, LANES), 1)
    pick = jnp.where((feat >= DIFF_HEAD_DIM) == (half == 1), 1.0, 0.0)
    pick = jnp.where(half < 2, pick, 0.0).astype(BF16)

    def sq_norms(x):
        xf = x.astype(F32)
        return lax.dot_general(pick, (xf * xf).astype(BF16), _NT, preferred_element_type=F32)

    @pl.when(qi == 0)
    def _():
        kmax_ref[0] = jnp.zeros((SUBLANES, LANES), F32)
        qpos = lax.broadcasted_iota(jnp.int32, (t, tq), 1)
        for d in range(n_diag):
            kpos = lax.broadcasted_iota(jnp.int32, (t, tq), 0) + d * t
            ahead = jnp.maximum(kpos - qpos, 0).astype(F32)
            bias_ref[d] = jnp.where(kpos // CHUNK <= qpos // CHUNK, -2.0 * slope * ahead, NEG_INF)

    eye = jnp.where(lax.broadcasted_iota(jnp.int32, (LANES, LANES), 0)
                    == lax.broadcasted_iota(jnp.int32, (LANES, LANES), 1), 1.0, 0.0).astype(BF16)
    for d in range(n_diag):
        j = qi * n_diag + d
        rows = pl.ds(pl.multiple_of(j * t, t), t)
        vt = lax.dot_general(eye, v_ref[0, rows, :], _NT, preferred_element_type=F32)
        vt_ref[j, 0:LANES, :] = vt.astype(BF16)
        vt_ref[j, LANES:LANES + ROWS_ONES, :] = jnp.ones((ROWS_ONES, t), BF16)
        k_sq = jnp.max(sq_norms(k_ref[0, rows, :]), axis=1, keepdims=True)
        kmax_ref[j + 1] = jnp.maximum(kmax_ref[j], k_sq)

    sub = lax.broadcasted_iota(jnp.int32, (t, LANES), 0)
    q_raw = q_ref[0]
    q = q_raw.astype(F32) * (DIFF_HEAD_DIM ** -0.5)
    one_col = jnp.where(lane == 0, 1.0, 0.0).astype(BF16)
    q1 = jnp.concatenate([jnp.where(lane < DIFF_HEAD_DIM, q, 0.0).astype(BF16), one_col], axis=1)
    q2 = jnp.concatenate([jnp.where(lane < DIFF_HEAD_DIM, 0.0, q).astype(BF16), one_col], axis=1)
    k_aug = jnp.where(lax.broadcasted_iota(jnp.int32, (t, LANES), 1) == 0,
                      slope * sub.astype(F32), 0.0).astype(BF16)
    q_pos = lax.broadcasted_iota(jnp.int32, (1, tq), 1) + qi * tq
    q_sq = sq_norms(q_raw) * (BOUND_SLACK / DIFF_HEAD_DIM)

    def key_tile(j):
        k2 = k_ref[0, pl.ds(pl.multiple_of(j * t, t), t), :]
        return jnp.concatenate([k2, k_aug], axis=1)

    def query_bias(j):
        return -slope * (q_pos - j * t).astype(F32)

    def update(idx, scores, biases, vt, m):
        m_new = m
        for s, c in zip(scores, biases):
            m_new = jnp.maximum(m_new, jnp.max(s, axis=0, keepdims=True) + c)
        alpha = jnp.exp(m - m_new)
        p = jnp.concatenate([jnp.exp(s - (m_new - c)).astype(BF16) for s, c in zip(scores, biases)], axis=0)
        acc_ref[idx] = alpha * acc_ref[idx] + jnp.dot(vt, p, preferred_element_type=F32)
        return m_new

    def accumulate(idx, scores, biases, vt, m):
        p = jnp.concatenate([jnp.exp(s - (m - c)).astype(BF16) for s, c in zip(scores, biases)], axis=0)
        acc_ref[idx] += jnp.dot(vt, p, preferred_element_type=F32)
        return m

    def group(j_top, m1, m2, diagonal, step):
        js = [jnp.maximum(j_top - g, 0) for g in range(DIFF_GROUP)]
        cs = [jnp.where(jnp.full((1, tq), j_top - g, jnp.int32) >= 0, query_bias(js[g]), NEG_INF)
              for g in range(DIFF_GROUP)]
        ks = [key_tile(j) for j in js]
        vt = jnp.concatenate([vt_ref[j] for j in js], axis=1)
        s1 = [lax.dot_general(k, q1, _NT, preferred_element_type=F32) for k in ks]
        s2 = [lax.dot_general(k, q2, _NT, preferred_element_type=F32) for k in ks]
        if diagonal:
            for g in range(n_diag):
                s1[g] = s1[g] + bias_ref[n_diag - 1 - g]
                s2[g] = s2[g] + bias_ref[n_diag - 1 - g]
        return step(0, s1, cs, vt, m1), step(1, s2, cs, vt, m2)

    def headroom(j_top, m1, m2):
        k_sq = kmax_ref[jnp.maximum(j_top, 0) + 1]
        reach = jnp.sqrt(q_sq * k_sq[:, 0:1])
        nearest = -slope * (q_pos - ((j_top + 1) * t - 1)).astype(F32)
        return jnp.max(jnp.maximum(reach[0:1] - m1, reach[1:2] - m2) + nearest)

    acc_ref[...] = jnp.zeros(acc_ref.shape, F32)
    m0 = jnp.full((1, tq), NEG_INF, F32)
    j_diag = (qi + 1) * n_diag - 1
    m1, m2 = group(j_diag, m0, m0, True, update)

    def cond(c):
        return jnp.logical_and(c[0] >= 0, c[1] > -EXP_ZERO_BELOW)

    def body(c):
        j_top, room, m1, m2 = c
        j_next = j_top - DIFF_GROUP

        def fast():
            return (headroom(j_next, m1, m2),) + group(j_top, m1, m2, False, accumulate)

        def exact():
            return (headroom(j_next, m1, m2),) + group(j_top, m1, m2, False, update)

        room, m1, m2 = lax.cond(room < FAST_BELOW, fast, exact)
        return j_next, room, m1, m2

    j_first = j_diag - DIFF_GROUP
    lax.while_loop(cond, body, (j_first, headroom(j_first, m1, m2), m1, m2))

    lam = lam_ref[...]
    lam_full = (jnp.exp(jnp.sum(lam[0:1] * lam[1:2], keepdims=True))
                - jnp.exp(jnp.sum(lam[2:3] * lam[3:4], keepdims=True)) + lam_init)
    a1 = acc_ref[0]
    a2 = acc_ref[1]
    out_t = (a1[0:LANES] / a1[LANES:LANES + 1]
             - lam_full * (a2[0:LANES] / a2[LANES:LANES + 1]))
    out = out_t.T
    ms = jnp.mean(out * out, axis=-1, keepdims=True)
    out = out * lax.rsqrt(ms + RMS_EPS) * g_ref[...] * (1.0 - lam_init)
    o_ref[0] = out.astype(o_ref.dtype)


def _diff_attention(u, d, lam, g_sub, lam_init):
    b, s, _ = u.shape
    t = ATT_TILE
    tq = DIFF_Q_TILE
    n_heads = d // LANES
    return pl.pallas_call(
        functools.partial(_diff_kernel, t=t, tq=tq, n_heads=n_heads, lam_init=lam_init),
        grid=(b, n_heads, s // tq),
        in_specs=[
            pl.BlockSpec((1, tq, LANES), lambda i, h, j: (i, j, h)),
            pl.BlockSpec((1, s, LANES), lambda i, h, j: (i, 0, n_heads + h)),
            pl.BlockSpec((1, s, LANES), lambda i, h, j: (i, 0, 2 * n_heads + h)),
            pl.BlockSpec(lam.shape, lambda i, h, j: (0, 0)),
            pl.BlockSpec((1, LANES), lambda i, h, j: (0, 0)),
        ],
        out_specs=pl.BlockSpec((1, tq, LANES), lambda i, h, j: (i, j, h)),
        out_shape=jax.ShapeDtypeStruct((b, s, d), BF16),
        scratch_shapes=[pltpu.VMEM((s // t, LANES + ROWS_ONES, t), BF16),
                        pltpu.VMEM((s // t + 1, SUBLANES, LANES), F32),
                        pltpu.VMEM((tq // t, t, tq), F32),
                        pltpu.VMEM((2, LANES + ROWS_ONES, tq), F32)],
        compiler_params=pltpu.CompilerParams(
            dimension_semantics=("arbitrary", "arbitrary", "arbitrary"),
            vmem_limit_bytes=VMEM_LIMIT),
        name="diff_attn",
    )(u, u, u, lam, g_sub.reshape(1, LANES))


def kernel(x, c, w_ada, b_ada, g_pre, g_post, w_in, w_out, diff_lambda, diff_subln):
    depth = w_ada.shape[0]
    d = x.shape[-1]
    assert w_in.shape[2] == 4 * d and d % LANES == 0
    assert x.shape[1] % ROW_TILE == 0 and x.shape[1] % DIFF_Q_TILE == 0
    assert DIFF_Q_TILE % ATT_TILE == 0 and ATT_TILE % CHUNK == 0
    ada = _ada(c, w_ada, b_ada)
    w_in_b = w_in.astype(BF16)
    w_out_b = w_out.astype(BF16)
    for i in range(depth):
        u = _pre(x, g_pre[i], ada[i], w_in_b[i])
        if i % N_MIXERS == 0:
            y = _sb_attention(u, d)
        else:
            j = i // N_MIXERS
            y = _diff_attention(u, d, diff_lambda[j], diff_subln[j], _diff_lambda_init(i))
        x = _post(y, u, x, g_post[i], ada[i], w_out_b[i])
    return x
```

```python
import functools
import math

import jax
import jax.numpy as jnp
from jax import lax
from jax.experimental import pallas as pl
from jax.experimental.pallas import tpu as pltpu

F32 = jnp.float32
BF16 = jnp.bfloat16

LANES = 128
SUBLANES = 8
N_MIXERS = 2
CHUNK = 64
SB_HEAD_DIM = 64
DIFF_HEAD_DIM = 64
ALIBI_MAX_EXP = 8.0
RMS_EPS = 1e-6
NEG_INF = -1e30
LOG2E = 1.4426950408889634
SB_LOGIT_CAP = 126.0
SB_EXIT_BITS = 160.0
DIFF_GROUP = 4
EXP_ZERO_BELOW = 104.0
FAST_BELOW = 60.0
BOUND_SLACK = 1.05
ROWS_ONES = 16

ROW_TILE = 512
ATT_TILE = 256
SB_Q_TILES = 2
DIFF_Q_TILE = 1024
VMEM_LIMIT = 48 * 1024 * 1024

_NT = (((1,), (1,)), ((), ()))


def _diff_lambda_init(layer_idx):
    return 0.8 - 0.6 * math.exp(-0.3 * layer_idx)


def _ada_kernel(c_ref, w_ref, b_ref, o_ref):
    c = c_ref[...]
    cond = c * jax.nn.sigmoid(c)
    o_ref[0] = jnp.dot(cond, w_ref[0], preferred_element_type=F32,
                       precision=lax.Precision.HIGHEST) + b_ref[0]


def _ada(c, w_ada, b_ada):
    depth, d, d3 = w_ada.shape
    nb = c.shape[0]
    b = -(-nb // SUBLANES) * SUBLANES
    c = jnp.pad(c, ((0, b - nb), (0, 0)))
    nt = d3 // d
    out = pl.pallas_call(
        _ada_kernel,
        grid=(depth, nt),
        in_specs=[
            pl.BlockSpec((b, d), lambda i, j: (0, 0)),
            pl.BlockSpec((1, d, d), lambda i, j: (i, 0, j)),
            pl.BlockSpec((1, 1, d), lambda i, j: (i, 0, j)),
        ],
        out_specs=pl.BlockSpec((1, b, d), lambda i, j: (i, 0, j)),
        out_shape=jax.ShapeDtypeStruct((depth, b, d3), F32),
        name="ada",
    )(c, w_ada, b_ada.reshape(depth, 1, d3))
    return out[:, :nb]


def _pre_kernel(x_ref, g_ref, ada_ref, w_ref, u_ref, *, d):
    x = x_ref[0]
    ms = jnp.mean(x * x, axis=-1, keepdims=True)
    y = x * lax.rsqrt(ms + RMS_EPS) * g_ref[...]
    ada = ada_ref[0]
    h = (y * (1.0 + ada[:, d:2 * d]) + ada[:, 0:d]).astype(BF16)
    n_out = w_ref.shape[1]
    for n in range(n_out // d):
        u_ref[0, :, n * d:(n + 1) * d] = jnp.dot(
            h, w_ref[:, n * d:(n + 1) * d], preferred_element_type=F32).astype(BF16)


def _pre(x, g_pre, ada, w_in):
    b, s, d = x.shape
    n_out = w_in.shape[1]
    tm = ROW_TILE
    return pl.pallas_call(
        functools.partial(_pre_kernel, d=d),
        grid=(b, s // tm),
        in_specs=[
            pl.BlockSpec((1, tm, d), lambda i, j: (i, j, 0)),
            pl.BlockSpec((1, d), lambda i, j: (0, 0)),
            pl.BlockSpec((1, 1, 3 * d), lambda i, j: (i, 0, 0)),
            pl.BlockSpec((d, n_out), lambda i, j: (0, 0)),
        ],
        out_specs=pl.BlockSpec((1, tm, n_out), lambda i, j: (i, j, 0)),
        out_shape=jax.ShapeDtypeStruct((b, s, n_out), BF16),
        compiler_params=pltpu.CompilerParams(
            dimension_semantics=("arbitrary", "arbitrary"), vmem_limit_bytes=VMEM_LIMIT),
        name="pre",
    )(x, g_pre.reshape(1, d), ada.reshape(b, 1, 3 * d), w_in)


def _post_kernel(y_ref, z_ref, x_ref, g_ref, ada_ref, w_ref, o_ref, *, d):
    y = y_ref[0].astype(F32)
    z = z_ref[0].astype(F32)
    t = (y * (z * jax.nn.sigmoid(z))).astype(BF16)
    r = jnp.dot(t, w_ref[...], preferred_element_type=F32)
    ms = jnp.mean(r * r, axis=-1, keepdims=True)
    rn = r * lax.rsqrt(ms + RMS_EPS) * g_ref[...]
    gate = ada_ref[0][:, 2 * d:3 * d]
    o_ref[0] = x_ref[0] + gate * rn


def _post(y, u, x, g_post, ada, w_out):
    b, s, d = x.shape
    tm = ROW_TILE
    z_col = u.shape[2] // d - 1
    return pl.pallas_call(
        functools.partial(_post_kernel, d=d),
        grid=(b, s // tm),
        in_specs=[
            pl.BlockSpec((1, tm, d), lambda i, j: (i, j, 0)),
            pl.BlockSpec((1, tm, d), lambda i, j: (i, j, z_col)),
            pl.BlockSpec((1, tm, d), lambda i, j: (i, j, 0)),
            pl.BlockSpec((1, d), lambda i, j: (0, 0)),
            pl.BlockSpec((1, 1, 3 * d), lambda i, j: (i, 0, 0)),
            pl.BlockSpec((d, d), lambda i, j: (0, 0)),
        ],
        out_specs=pl.BlockSpec((1, tm, d), lambda i, j: (i, j, 0)),
        out_shape=jax.ShapeDtypeStruct((b, s, d), F32),
        compiler_params=pltpu.CompilerParams(
            dimension_semantics=("arbitrary", "arbitrary"), vmem_limit_bytes=VMEM_LIMIT),
        name="post",
    )(y, u, x, g_post.reshape(1, d), ada.reshape(b, 1, 3 * d), w_out)


def _sb_kernel(q_ref, k_ref, v_ref, o_ref, vt_ref, acc_ref, *, t, n_sub):
    eye = jnp.where(lax.broadcasted_iota(jnp.int32, (LANES, LANES), 0)
                    == lax.broadcasted_iota(jnp.int32, (LANES, LANES), 1), 1.0, 0.0).astype(BF16)
    for sub in range(n_sub):
        j = pl.program_id(2) * n_sub + sub
        v_tile = v_ref[0, pl.ds(pl.multiple_of(j * t, t), t), :]
        vt_ref[j] = lax.dot_general(eye, v_tile, _NT, preferred_element_type=F32).astype(BF16)

    lane = lax.broadcasted_iota(jnp.int32, (t, LANES), 1)
    kpos = lax.broadcasted_iota(jnp.int32, (t, 2 * t), 0)
    qpos = lax.broadcasted_iota(jnp.int32, (t, 2 * t), 1) % t
    strict = kpos < qpos
    upper = jnp.where(lax.broadcasted_iota(jnp.int32, (t, t), 1)
                      > lax.broadcasted_iota(jnp.int32, (t, t), 0), 1.0, 0.0).astype(BF16)

    def logits(qs, j):
        rows = pl.ds(pl.multiple_of(j * t, t), t)
        return lax.dot_general(k_ref[0, rows, :], qs, _NT, preferred_element_type=F32)

    def soften(z, diag):
        z = jnp.minimum(z, SB_LOGIT_CAP)
        drop = jnp.log2(1.0 + jnp.exp2(z))
        log_beta = z - drop
        return (jnp.where(strict, drop, 0.0) if diag else drop), log_beta

    def suffix(drop):
        return jnp.dot(upper, drop.astype(BF16), preferred_element_type=F32)

    def weigh(j, log_beta, between, carry, diag):
        w = jnp.exp2(log_beta - between - carry)
        if diag:
            w = jnp.where(strict, w, 0.0)
        return jnp.dot(vt_ref[j], w.astype(BF16), preferred_element_type=F32)

    def live(carry):
        return jnp.min(carry) < SB_EXIT_BITS

    work = []
    for sub in range(n_sub):
        qt = pl.program_id(2) * n_sub + sub
        q = q_ref[0, sub * t:(sub + 1) * t, :].astype(F32) * ((SB_HEAD_DIM ** -0.5) * LOG2E)
        qs = jnp.concatenate([jnp.where(lane < SB_HEAD_DIM, q, 0.0).astype(BF16),
                              jnp.where(lane < SB_HEAD_DIM, 0.0, q).astype(BF16)], axis=0)
        j_prev = jnp.maximum(qt - 1, 0)
        work.append(dict(sub=sub, qt=qt, qs=qs, j_prev=j_prev,
                         z_d=logits(qs, qt), z_p=logits(qs, j_prev)))
    for w_ in work:
        w_["drop_d"], w_["lb_d"] = soften(w_["z_d"], True)
        w_["btw_d"] = suffix(w_["drop_d"])
        w_["drop_p"], w_["lb_p"] = soften(w_["z_p"], False)
        w_["btw_p"] = suffix(w_["drop_p"])
    for w_ in work:
        carry = w_["btw_d"][0:1, :] + w_["drop_d"][0:1, :]
        has_prev = jnp.full((1, 2 * t), w_["qt"], jnp.int32) >= 1
        carry = jnp.where(has_prev, carry, -NEG_INF)
        pv = (weigh(w_["qt"], w_["lb_d"], w_["btw_d"], jnp.zeros((1, 2 * t), F32), True)
              + weigh(w_["j_prev"], w_["lb_p"], w_["btw_p"], carry, False))
        acc_ref[w_["sub"]] = pv
        w_["carry"] = carry + w_["btw_p"][0:1, :] + w_["drop_p"][0:1, :]

    for w_ in work:
        def cond(c):
            return jnp.logical_and(c[0] >= 0, c[1])

        def body(c, sub=w_["sub"], qs=w_["qs"]):
            j, _, carry = c
            drop, log_beta = soften(logits(qs, j), False)
            between = suffix(drop)
            acc_ref[sub] += weigh(j, log_beta, between, carry, False)
            carry = carry + between[0:1, :] + drop[0:1, :]
            return j - 1, live(carry), carry

        lax.while_loop(cond, body, (w_["qt"] - 2, live(w_["carry"]), w_["carry"]))
        acc = acc_ref[w_["sub"]]
        out_t = jnp.concatenate([acc[0:SB_HEAD_DIM, 0:t], acc[SB_HEAD_DIM:LANES, t:2 * t]], axis=0)
        o_ref[0, w_["sub"] * t:(w_["sub"] + 1) * t, :] = out_t.T.astype(o_ref.dtype)


def _sb_attention(u, d):
    b, s, _ = u.shape
    t = ATT_TILE
    n_sub = SB_Q_TILES
    npair = d // LANES
    return pl.pallas_call(
        functools.partial(_sb_kernel, t=t, n_sub=n_sub),
        grid=(b, npair, s // (t * n_sub)),
        in_specs=[
            pl.BlockSpec((1, t * n_sub, LANES), lambda i, h, j: (i, j, h)),
            pl.BlockSpec((1, s, LANES), lambda i, h, j: (i, 0, npair + h)),
            pl.BlockSpec((1, s, LANES), lambda i, h, j: (i, 0, 2 * npair + h)),
        ],
        out_specs=pl.BlockSpec((1, t * n_sub, LANES), lambda i, h, j: (i, j, h)),
        out_shape=jax.ShapeDtypeStruct((b, s, d), BF16),
        scratch_shapes=[pltpu.VMEM((s // t, LANES, t), BF16),
                        pltpu.VMEM((n_sub, LANES, 2 * t), F32)],
        compiler_params=pltpu.CompilerParams(
            dimension_semantics=("arbitrary", "arbitrary", "arbitrary"),
            vmem_limit_bytes=VMEM_LIMIT),
        name="sb_attn",
    )(u, u, u)


def _diff_kernel(q_ref, k_ref, v_ref, lam_ref, g_ref, o_ref, vt_ref, kmax_ref, bias_ref, acc_ref,
                 *, t, tq, n_heads, lam_init):
    h = pl.program_id(1)
    qi = pl.program_id(2)
    head_no = (lax.broadcasted_iota(jnp.int32, (1, 1), 0) + (h + 1)).astype(F32)
    slope = jnp.exp2(-head_no * (ALIBI_MAX_EXP / n_heads))
    lane = lax.broadcasted_iota(jnp.int32, (tq, LANES), 1)
    n_diag = tq // t
    half = lax.broadcasted_iota(jnp.int32, (SUBLANES, LANES), 0)
    feat = lax.broadcasted_iota(jnp.int32, (SUBLANES, LANES), 1)
    pick = jnp.where((feat >= DIFF_HEAD_DIM) == (half == 1), 1.0, 0.0)
    pick = jnp.where(half < 2, pick, 0.0).astype(BF16)

    def sq_norms(x):
        xf = x.astype(F32)
        return lax.dot_general(pick, (xf * xf).astype(BF16), _NT, preferred_element_type=F32)

    @pl.when(qi == 0)
    def _():
        kmax_ref[0] = jnp.zeros((SUBLANES, LANES), F32)
        kpos = lax.broadcasted_iota(jnp.int32, (t, t), 0)
        qpos = lax.broadcasted_iota(jnp.int32, (t, t), 1)
        ahead = jnp.maximum(kpos - qpos, 0).astype(F32)
        bias_ref[...] = jnp.where(kpos // CHUNK <= qpos // CHUNK, -2.0 * slope * ahead, NEG_INF)

    eye = jnp.where(lax.broadcasted_iota(jnp.int32, (LANES, LANES), 0)
                    == lax.broadcasted_iota(jnp.int32, (LANES, LANES), 1), 1.0, 0.0).astype(BF16)
    for d in range(n_diag):
        j = qi * n_diag + d
        rows = pl.ds(pl.multiple_of(j * t, t), t)
        vt = lax.dot_general(eye, v_ref[0, rows, :], _NT, preferred_element_type=F32)
        vt_ref[j, 0:LANES, :] = vt.astype(BF16)
        vt_ref[j, LANES:LANES + ROWS_ONES, :] = jnp.ones((ROWS_ONES, t), BF16)
        k_sq = jnp.max(sq_norms(k_ref[0, rows, :]), axis=1, keepdims=True)
        kmax_ref[j + 1] = jnp.maximum(kmax_ref[j], k_sq)

    sub = lax.broadcasted_iota(jnp.int32, (t, LANES), 0)
    q_raw = q_ref[0]
    q = q_raw.astype(F32) * (DIFF_HEAD_DIM ** -0.5)
    one_col = jnp.where(lane == 0, 1.0, 0.0).astype(BF16)
    q1 = jnp.concatenate([jnp.where(lane < DIFF_HEAD_DIM, q, 0.0).astype(BF16), one_col], axis=1)
    q2 = jnp.concatenate([jnp.where(lane < DIFF_HEAD_DIM, 0.0, q).astype(BF16), one_col], axis=1)
    k_aug = jnp.where(lax.broadcasted_iota(jnp.int32, (t, LANES), 1) == 0,
                      slope * sub.astype(F32), 0.0).astype(BF16)
    q_pos = lax.broadcasted_iota(jnp.int32, (1, tq), 1) + qi * tq
    q_sq = sq_norms(q_raw) * (BOUND_SLACK / DIFF_HEAD_DIM)

    def key_tile(j):
        k2 = k_ref[0, pl.ds(pl.multiple_of(j * t, t), t), :]
        return jnp.concatenate([k2, k_aug], axis=1)

    def query_bias(j):
        return -slope * (q_pos - j * t).astype(F32)

    def update(idx, scores, biases, vt, m):
        m_new = m
        for s, c in zip(scores, biases):
            m_new = jnp.maximum(m_new, jnp.max(s, axis=0, keepdims=True) + c)
        alpha = jnp.exp(m - m_new)
        p = jnp.concatenate([jnp.exp(s - (m_new - c)).astype(BF16) for s, c in zip(scores, biases)], axis=0)
        acc_ref[idx] = alpha * acc_ref[idx] + jnp.dot(vt, p, preferred_element_type=F32)
        return m_new

    def accumulate(idx, scores, biases, vt, m):
        p = jnp.concatenate([jnp.exp(s - (m - c)).astype(BF16) for s, c in zip(scores, biases)], axis=0)
        acc_ref[idx] += jnp.dot(vt, p, preferred_element_type=F32)
        return m

    def group(j_top, m1, m2, step):
        js = [jnp.maximum(j_top - g, 0) for g in range(DIFF_GROUP)]
        cs = [jnp.where(jnp.full((1, tq), j_top - g, jnp.int32) >= 0, query_bias(js[g]), NEG_INF)
              for g in range(DIFF_GROUP)]
        ks = [key_tile(j) for j in js]
        vt = jnp.concatenate([vt_ref[j] for j in js], axis=1)
        s1 = [lax.dot_general(k, q1, _NT, preferred_element_type=F32) for k in ks]
        s2 = [lax.dot_general(k, q2, _NT, preferred_element_type=F32) for k in ks]
        return step(0, s1, cs, vt, m1), step(1, s2, cs, vt, m2)

    def diagonal(idx, qm):
        js = [qi * n_diag + d for d in range(n_diag)]
        scores, shifts = [], []
        for d, j in enumerate(js):
            s = lax.dot_general(key_tile(j), qm[d * t:, :], _NT, preferred_element_type=F32)
            scores.append(jnp.concatenate([s[:, 0:t] + bias_ref[...], s[:, t:]], axis=1) if d < n_diag - 1
                          else s + bias_ref[...])
            shifts.append(query_bias(j)[:, d * t:])
        m = jnp.concatenate(
            [functools.reduce(jnp.maximum, [jnp.max(scores[d][:, (b - d) * t:(b - d + 1) * t], axis=0, keepdims=True)
                                            + shifts[d][:, (b - d) * t:(b - d + 1) * t] for d in range(b + 1)])
             for b in range(n_diag)], axis=1)
        for d, j in enumerate(js):
            p = jnp.exp(scores[d] - (m[:, d * t:] - shifts[d])).astype(BF16)
            pv = jnp.dot(vt_ref[j], p, preferred_element_type=F32)
            if d == 0:
                acc_ref[idx] = pv
            else:
                acc_ref[idx, :, d * t:] += pv
        return m

    def headroom(j_top, m1, m2):
        k_sq = kmax_ref[jnp.maximum(j_top, 0) + 1]
        reach = jnp.sqrt(q_sq * k_sq[:, 0:1])
        nearest = -slope * (q_pos - ((j_top + 1) * t - 1)).astype(F32)
        return jnp.max(jnp.maximum(reach[0:1] - m1, reach[1:2] - m2) + nearest)

    m1 = diagonal(0, q1)
    m2 = diagonal(1, q2)

    def cond(c):
        return jnp.logical_and(c[0] >= 0, c[1] > -EXP_ZERO_BELOW)

    def body(c):
        j_top, room, m1, m2 = c
        j_next = j_top - DIFF_GROUP

        def fast():
            return (headroom(j_next, m1, m2),) + group(j_top, m1, m2, accumulate)

        def exact():
            return (headroom(j_next, m1, m2),) + group(j_top, m1, m2, update)

        room, m1, m2 = lax.cond(room < FAST_BELOW, fast, exact)
        return j_next, room, m1, m2

    j_first = qi * n_diag - 1
    lax.while_loop(cond, body, (j_first, headroom(j_first, m1, m2), m1, m2))

    lam = lam_ref[...]
    lam_full = (jnp.exp(jnp.sum(lam[0:1] * lam[1:2], keepdims=True))
                - jnp.exp(jnp.sum(lam[2:3] * lam[3:4], keepdims=True)) + lam_init)
    a1 = acc_ref[0]
    a2 = acc_ref[1]
    out_t = (a1[0:LANES] / a1[LANES:LANES + 1]
             - lam_full * (a2[0:LANES] / a2[LANES:LANES + 1]))
    out = out_t.T
    ms = jnp.mean(out * out, axis=-1, keepdims=True)
    out = out * lax.rsqrt(ms + RMS_EPS) * g_ref[...] * (1.0 - lam_init)
    o_ref[0] = out.astype(o_ref.dtype)


def _diff_attention(u, d, lam, g_sub, lam_init):
    b, s, _ = u.shape
    t = ATT_TILE
    tq = DIFF_Q_TILE
    n_heads = d // LANES
    return pl.pallas_call(
        functools.partial(_diff_kernel, t=t, tq=tq, n_heads=n_heads, lam_init=lam_init),
        grid=(b, n_heads, s // tq),
        in_specs=[
            pl.BlockSpec((1, tq, LANES), lambda i, h, j: (i, j, h)),
            pl.BlockSpec((1, s, LANES), lambda i, h, j: (i, 0, n_heads + h)),
            pl.BlockSpec((1, s, LANES), lambda i, h, j: (i, 0, 2 * n_heads + h)),
            pl.BlockSpec(lam.shape, lambda i, h, j: (0, 0)),
            pl.BlockSpec((1, LANES), lambda i, h, j: (0, 0)),
        ],
        out_specs=pl.BlockSpec((1, tq, LANES), lambda i, h, j: (i, j, h)),
        out_shape=jax.ShapeDtypeStruct((b, s, d), BF16),
        scratch_shapes=[pltpu.VMEM((s // t, LANES + ROWS_ONES, t), BF16),
                        pltpu.VMEM((s // t + 1, SUBLANES, LANES), F32),
                        pltpu.VMEM((t, t), F32),
                        pltpu.VMEM((2, LANES + ROWS_ONES, tq), F32)],
        compiler_params=pltpu.CompilerParams(
            dimension_semantics=("arbitrary", "arbitrary", "arbitrary"),
            vmem_limit_bytes=VMEM_LIMIT),
        name="diff_attn",
    )(u, u, u, lam, g_sub.reshape(1, LANES))


def kernel(x, c, w_ada, b_ada, g_pre, g_post, w_in, w_out, diff_lambda, diff_subln):
    depth = w_ada.shape[0]
    d = x.shape[-1]
    assert w_in.shape[2] == 4 * d and d % LANES == 0
    assert x.shape[1] % ROW_TILE == 0 and x.shape[1] % DIFF_Q_TILE == 0
    assert DIFF_Q_TILE % ATT_TILE == 0 and ATT_TILE % CHUNK == 0
    ada = _ada(c, w_ada, b_ada)
    w_in_b = w_in.astype(BF16)
    w_out_b = w_out.astype(BF16)
    for i in range(depth):
        u = _pre(x, g_pre[i], ada[i], w_in_b[i])
        if i % N_MIXERS == 0:
            y = _sb_attention(u, d)
        else:
            j = i // N_MIXERS
            y = _diff_attention(u, d, diff_lambda[j], diff_subln[j], _diff_lambda_init(i))
        x = _post(y, u, x, g_post[i], ada[i], w_out_b[i])
    return x
```

```python
import functools
import math

import jax
import jax.numpy as jnp
from jax import lax
from jax.experimental import pallas as pl
from jax.experimental.pallas import tpu as pltpu

F32 = jnp.float32
BF16 = jnp.bfloat16

LANES = 128
SUBLANES = 8
N_MIXERS = 2
CHUNK = 64
SB_HEAD_DIM = 64
DIFF_HEAD_DIM = 64
ALIBI_MAX_EXP = 8.0
RMS_EPS = 1e-6
NEG_INF = -1e30
LOG2E = 1.4426950408889634
SB_LOGIT_CAP = 126.0
SB_EXIT_BITS = 160.0
DIFF_GROUP = 4
EXP_ZERO_BELOW = 104.0
FAST_BELOW = 60.0
BOUND_SLACK = 1.05
ROWS_ONES = 16

ROW_TILE = 512
ATT_TILE = 256
SB_Q_TILES = 2
DIFF_Q_TILE = 1024
VMEM_LIMIT = 48 * 1024 * 1024

_NT = (((1,), (1,)), ((), ()))


def _diff_lambda_init(layer_idx):
    return 0.8 - 0.6 * math.exp(-0.3 * layer_idx)


def _ada_kernel(c_ref, w_ref, b_ref, o_ref):
    c = c_ref[...]
    cond = c * jax.nn.sigmoid(c)
    o_ref[0] = jnp.dot(cond, w_ref[0], preferred_element_type=F32,
                       precision=lax.Precision.HIGHEST) + b_ref[0]


def _ada(c, w_ada, b_ada):
    depth, d, d3 = w_ada.shape
    nb = c.shape[0]
    b = -(-nb // SUBLANES) * SUBLANES
    c = jnp.pad(c, ((0, b - nb), (0, 0)))
    nt = d3 // d
    out = pl.pallas_call(
        _ada_kernel,
        grid=(depth, nt),
        in_specs=[
            pl.BlockSpec((b, d), lambda i, j: (0, 0)),
            pl.BlockSpec((1, d, d), lambda i, j: (i, 0, j)),
            pl.BlockSpec((1, 1, d), lambda i, j: (i, 0, j)),
        ],
        out_specs=pl.BlockSpec((1, b, d), lambda i, j: (i, 0, j)),
        out_shape=jax.ShapeDtypeStruct((depth, b, d3), F32),
        name="ada",
    )(c, w_ada, b_ada.reshape(depth, 1, d3))
    return out[:, :nb]


def _project(x, g_ref, ada_ref, w_ref, u_ref, d):
    ms = jnp.mean(x * x, axis=-1, keepdims=True)
    y = x * lax.rsqrt(ms + RMS_EPS) * g_ref[...]
    ada = ada_ref[0]
    h = (y * (1.0 + ada[:, d:2 * d]) + ada[:, 0:d]).astype(BF16)
    n_out = w_ref.shape[1]
    for n in range(n_out // d):
        u_ref[0, :, n * d:(n + 1) * d] = jnp.dot(
            h, w_ref[:, n * d:(n + 1) * d], preferred_element_type=F32).astype(BF16)


def _residual(y_ref, z_ref, x_ref, g_ref, ada_ref, w_ref, d):
    y = y_ref[0].astype(F32)
    z = z_ref[0].astype(F32)
    t = (y * (z * jax.nn.sigmoid(z))).astype(BF16)
    r = jnp.dot(t, w_ref[...], preferred_element_type=F32)
    ms = jnp.mean(r * r, axis=-1, keepdims=True)
    rn = r * lax.rsqrt(ms + RMS_EPS) * g_ref[...]
    gate = ada_ref[0][:, 2 * d:3 * d]
    return x_ref[0] + gate * rn


def _pre_kernel(x_ref, g_ref, ada_ref, w_ref, u_ref, *, d):
    _project(x_ref[0], g_ref, ada_ref, w_ref, u_ref, d)


def _post_kernel(y_ref, z_ref, x_ref, g_ref, ada_ref, w_ref, o_ref, *, d):
    o_ref[0] = _residual(y_ref, z_ref, x_ref, g_ref, ada_ref, w_ref, d)


def _post_pre_kernel(y_ref, z_ref, x_ref, g_post_ref, ada_ref, w_out_ref, g_pre_ref, ada_next_ref, w_in_ref,
                     o_ref, u_ref, *, d):
    x = _residual(y_ref, z_ref, x_ref, g_post_ref, ada_ref, w_out_ref, d)
    o_ref[0] = x
    _project(x, g_pre_ref, ada_next_ref, w_in_ref, u_ref, d)


def _row_specs(tm, d):
    row = pl.BlockSpec((1, tm, d), lambda i, j: (i, j, 0))
    gain = pl.BlockSpec((1, d), lambda i, j: (0, 0))
    ada = pl.BlockSpec((1, 1, 3 * d), lambda i, j: (i, 0, 0))
    return row, gain, ada


_ROW_PARAMS = dict(compiler_params=pltpu.CompilerParams(
    dimension_semantics=("arbitrary", "arbitrary"), vmem_limit_bytes=VMEM_LIMIT))


def _pre(x, g_pre, ada, w_in):
    b, s, d = x.shape
    n_out = w_in.shape[1]
    tm = ROW_TILE
    row, gain, ada_spec = _row_specs(tm, d)
    return pl.pallas_call(
        functools.partial(_pre_kernel, d=d),
        grid=(b, s // tm),
        in_specs=[row, gain, ada_spec, pl.BlockSpec((d, n_out), lambda i, j: (0, 0))],
        out_specs=pl.BlockSpec((1, tm, n_out), lambda i, j: (i, j, 0)),
        out_shape=jax.ShapeDtypeStruct((b, s, n_out), BF16),
        name="pre", **_ROW_PARAMS,
    )(x, g_pre.reshape(1, d), ada.reshape(b, 1, 3 * d), w_in)


def _post(y, u, x, g_post, ada, w_out, nxt=None):
    b, s, d = x.shape
    tm = ROW_TILE
    z_col = u.shape[2] // d - 1
    row, gain, ada_spec = _row_specs(tm, d)
    in_specs = [row, pl.BlockSpec((1, tm, d), lambda i, j: (i, j, z_col)), row, gain, ada_spec,
                pl.BlockSpec((d, d), lambda i, j: (0, 0))]
    args = [y, u, x, g_post.reshape(1, d), ada.reshape(b, 1, 3 * d), w_out]
    if nxt is None:
        return pl.pallas_call(
            functools.partial(_post_kernel, d=d),
            grid=(b, s // tm), in_specs=in_specs, out_specs=row,
            out_shape=jax.ShapeDtypeStruct((b, s, d), F32),
            name="post", **_ROW_PARAMS,
        )(*args)
    g_pre, ada_next, w_in = nxt
    n_out = w_in.shape[1]
    return pl.pallas_call(
        functools.partial(_post_pre_kernel, d=d),
        grid=(b, s // tm),
        in_specs=in_specs + [gain, ada_spec, pl.BlockSpec((d, n_out), lambda i, j: (0, 0))],
        out_specs=[row, pl.BlockSpec((1, tm, n_out), lambda i, j: (i, j, 0))],
        out_shape=[jax.ShapeDtypeStruct((b, s, d), F32), jax.ShapeDtypeStruct((b, s, n_out), BF16)],
        name="post_pre", **_ROW_PARAMS,
    )(*args, g_pre.reshape(1, d), ada_next.reshape(b, 1, 3 * d), w_in)


def _sb_kernel(q_ref, k_ref, v_ref, o_ref, vt_ref, acc_ref, *, t, n_sub):
    eye = jnp.where(lax.broadcasted_iota(jnp.int32, (LANES, LANES), 0)
                    == lax.broadcasted_iota(jnp.int32, (LANES, LANES), 1), 1.0, 0.0).astype(BF16)
    for sub in range(n_sub):
        j = pl.program_id(2) * n_sub + sub
        v_tile = v_ref[0, pl.ds(pl.multiple_of(j * t, t), t), :]
        vt_ref[j] = lax.dot_general(eye, v_tile, _NT, preferred_element_type=F32).astype(BF16)

    lane = lax.broadcasted_iota(jnp.int32, (t, LANES), 1)
    kpos = lax.broadcasted_iota(jnp.int32, (t, 2 * t), 0)
    qpos = lax.broadcasted_iota(jnp.int32, (t, 2 * t), 1) % t
    strict = kpos < qpos
    upper = jnp.where(lax.broadcasted_iota(jnp.int32, (t, t), 1)
                      > lax.broadcasted_iota(jnp.int32, (t, t), 0), 1.0, 0.0).astype(BF16)

    def logits(qs, j):
        rows = pl.ds(pl.multiple_of(j * t, t), t)
        return lax.dot_general(k_ref[0, rows, :], qs, _NT, preferred_element_type=F32)

    def soften(z, diag):
        z = jnp.minimum(z, SB_LOGIT_CAP)
        drop = jnp.log2(1.0 + jnp.exp2(z))
        log_beta = z - drop
        return (jnp.where(strict, drop, 0.0) if diag else drop), log_beta

    def suffix(drop):
        return jnp.dot(upper, drop.astype(BF16), preferred_element_type=F32)

    def weigh(j, log_beta, between, carry, diag):
        w = jnp.exp2(log_beta - between - carry)
        if diag:
            w = jnp.where(strict, w, 0.0)
        return jnp.dot(vt_ref[j], w.astype(BF16), preferred_element_type=F32)

    def live(carry):
        return jnp.min(carry) < SB_EXIT_BITS

    work = []
    for sub in range(n_sub):
        qt = pl.program_id(2) * n_sub + sub
        q = q_ref[0, sub * t:(sub + 1) * t, :].astype(F32) * ((SB_HEAD_DIM ** -0.5) * LOG2E)
        qs = jnp.concatenate([jnp.where(lane < SB_HEAD_DIM, q, 0.0).astype(BF16),
                              jnp.where(lane < SB_HEAD_DIM, 0.0, q).astype(BF16)], axis=0)
        j_prev = jnp.maximum(qt - 1, 0)
        work.append(dict(sub=sub, qt=qt, qs=qs, j_prev=j_prev,
                         z_d=logits(qs, qt), z_p=logits(qs, j_prev)))
    for w_ in work:
        w_["drop_d"], w_["lb_d"] = soften(w_["z_d"], True)
        w_["btw_d"] = suffix(w_["drop_d"])
        w_["drop_p"], w_["lb_p"] = soften(w_["z_p"], False)
        w_["btw_p"] = suffix(w_["drop_p"])
    for w_ in work:
        carry = w_["btw_d"][0:1, :] + w_["drop_d"][0:1, :]
        has_prev = jnp.full((1, 2 * t), w_["qt"], jnp.int32) >= 1
        carry = jnp.where(has_prev, carry, -NEG_INF)
        pv = (weigh(w_["qt"], w_["lb_d"], w_["btw_d"], jnp.zeros((1, 2 * t), F32), True)
              + weigh(w_["j_prev"], w_["lb_p"], w_["btw_p"], carry, False))
        acc_ref[w_["sub"]] = pv
        w_["carry"] = carry + w_["btw_p"][0:1, :] + w_["drop_p"][0:1, :]

    for w_ in work:
        def cond(c):
            return jnp.logical_and(c[0] >= 0, c[1])

        def body(c, sub=w_["sub"], qs=w_["qs"]):
            j, _, carry = c
            drop, log_beta = soften(logits(qs, j), False)
            between = suffix(drop)
            acc_ref[sub] += weigh(j, log_beta, between, carry, False)
            carry = carry + between[0:1, :] + drop[0:1, :]
            return j - 1, live(carry), carry

        lax.while_loop(cond, body, (w_["qt"] - 2, live(w_["carry"]), w_["carry"]))
        acc = acc_ref[w_["sub"]]
        out_t = jnp.concatenate([acc[0:SB_HEAD_DIM, 0:t], acc[SB_HEAD_DIM:LANES, t:2 * t]], axis=0)
        o_ref[0, w_["sub"] * t:(w_["sub"] + 1) * t, :] = out_t.T.astype(o_ref.dtype)


def _sb_attention(u, d):
    b, s, _ = u.shape
    t = ATT_TILE
    n_sub = SB_Q_TILES
    npair = d // LANES
    return pl.pallas_call(
        functools.partial(_sb_kernel, t=t, n_sub=n_sub),
        grid=(b, npair, s // (t * n_sub)),
        in_specs=[
            pl.BlockSpec((1, t * n_sub, LANES), lambda i, h, j: (i, j, h)),
            pl.BlockSpec((1, s, LANES), lambda i, h, j: (i, 0, npair + h)),
            pl.BlockSpec((1, s, LANES), lambda i, h, j: (i, 0, 2 * npair + h)),
        ],
        out_specs=pl.BlockSpec((1, t * n_sub, LANES), lambda i, h, j: (i, j, h)),
        out_shape=jax.ShapeDtypeStruct((b, s, d), BF16),
        scratch_shapes=[pltpu.VMEM((s // t, LANES, t), BF16),
                        pltpu.VMEM((n_sub, LANES, 2 * t), F32)],
        compiler_params=pltpu.CompilerParams(
            dimension_semantics=("arbitrary", "arbitrary", "arbitrary"),
            vmem_limit_bytes=VMEM_LIMIT),
        name="sb_attn",
    )(u, u, u)


def _diff_kernel(q_ref, k_ref, v_ref, lam_ref, g_ref, o_ref, vt_ref, kmax_ref, bias_ref, acc_ref,
                 *, t, tq, n_heads, lam_init):
    h = pl.program_id(1)
    qi = pl.program_id(2)
    head_no = (lax.broadcasted_iota(jnp.int32, (1, 1), 0) + (h + 1)).astype(F32)
    slope = jnp.exp2(-head_no * (ALIBI_MAX_EXP / n_heads))
    lane = lax.broadcasted_iota(jnp.int32, (tq, LANES), 1)
    n_diag = tq // t
    half = lax.broadcasted_iota(jnp.int32, (SUBLANES, LANES), 0)
    feat = lax.broadcasted_iota(jnp.int32, (SUBLANES, LANES), 1)
    pick = jnp.where((feat >= DIFF_HEAD_DIM) == (half == 1), 1.0, 0.0)
    pick = jnp.where(half < 2, pick, 0.0).astype(BF16)

    def sq_norms(x):
        xf = x.astype(F32)
        return lax.dot_general(pick, (xf * xf).astype(BF16), _NT, preferred_element_type=F32)

    @pl.when(qi == 0)
    def _():
        kmax_ref[0] = jnp.zeros((SUBLANES, LANES), F32)
        kpos = lax.broadcasted_iota(jnp.int32, (t, t), 0)
        qpos = lax.broadcasted_iota(jnp.int32, (t, t), 1)
        ahead = jnp.maximum(kpos - qpos, 0).astype(F32)
        bias_ref[...] = jnp.where(kpos // CHUNK <= qpos // CHUNK, -2.0 * slope * ahead, NEG_INF)

    eye = jnp.where(lax.broadcasted_iota(jnp.int32, (LANES, LANES), 0)
                    == lax.broadcasted_iota(jnp.int32, (LANES, LANES), 1), 1.0, 0.0).astype(BF16)
    for d in range(n_diag):
        j = qi * n_diag + d
        rows = pl.ds(pl.multiple_of(j * t, t), t)
        vt = lax.dot_general(eye, v_ref[0, rows, :], _NT, preferred_element_type=F32)
        vt_ref[j, 0:LANES, :] = vt.astype(BF16)
        vt_ref[j, LANES:LANES + ROWS_ONES, :] = jnp.ones((ROWS_ONES, t), BF16)
        k_sq = jnp.max(sq_norms(k_ref[0, rows, :]), axis=1, keepdims=True)
        kmax_ref[j + 1] = jnp.maximum(kmax_ref[j], k_sq)

    sub = lax.broadcasted_iota(jnp.int32, (t, LANES), 0)
    q_raw = q_ref[0]
    q = q_raw.astype(F32) * (DIFF_HEAD_DIM ** -0.5)
    one_col = jnp.where(lane == 0, 1.0, 0.0).astype(BF16)
    q1 = jnp.concatenate([jnp.where(lane < DIFF_HEAD_DIM, q, 0.0).astype(BF16), one_col], axis=1)
    q2 = jnp.concatenate([jnp.where(lane < DIFF_HEAD_DIM, 0.0, q).astype(BF16), one_col], axis=1)
    k_aug = jnp.where(lax.broadcasted_iota(jnp.int32, (t, LANES), 1) == 0,
                      slope * sub.astype(F32), 0.0).astype(BF16)
    q_pos = lax.broadcasted_iota(jnp.int32, (1, tq), 1) + qi * tq
    q_sq = sq_norms(q_raw) * (BOUND_SLACK / DIFF_HEAD_DIM)

    def key_tile(j):
        k2 = k_ref[0, pl.ds(pl.multiple_of(j * t, t), t), :]
        return jnp.concatenate([k2, k_aug], axis=1)

    def query_bias(j):
        return -slope * (q_pos - j * t).astype(F32)

    def update(idx, scores, biases, vt, m):
        m_new = m
        for s, c in zip(scores, biases):
            m_new = jnp.maximum(m_new, jnp.max(s, axis=0, keepdims=True) + c)
        alpha = jnp.exp(m - m_new)
        p = jnp.concatenate([jnp.exp(s - (m_new - c)).astype(BF16) for s, c in zip(scores, biases)], axis=0)
        acc_ref[idx] = alpha * acc_ref[idx] + jnp.dot(vt, p, preferred_element_type=F32)
        return m_new

    def accumulate(idx, scores, biases, vt, m):
        p = jnp.concatenate([jnp.exp(s - (m - c)).astype(BF16) for s, c in zip(scores, biases)], axis=0)
        acc_ref[idx] += jnp.dot(vt, p, preferred_element_type=F32)
        return m

    def group(j_top, m1, m2, step):
        js = [jnp.maximum(j_top - g, 0) for g in range(DIFF_GROUP)]
        cs = [jnp.where(jnp.full((1, tq), j_top - g, jnp.int32) >= 0, query_bias(js[g]), NEG_INF)
              for g in range(DIFF_GROUP)]
        ks = [key_tile(j) for j in js]
        vt = jnp.concatenate([vt_ref[j] for j in js], axis=1)
        s1 = [lax.dot_general(k, q1, _NT, preferred_element_type=F32) for k in ks]
        s2 = [lax.dot_general(k, q2, _NT, preferred_element_type=F32) for k in ks]
        return step(0, s1, cs, vt, m1), step(1, s2, cs, vt, m2)

    def diagonal(idx, qm):
        js = [qi * n_diag + d for d in range(n_diag)]
        scores, shifts = [], []
        for d, j in enumerate(js):
            s = lax.dot_general(key_tile(j), qm[d * t:, :], _NT, preferred_element_type=F32)
            scores.append(jnp.concatenate([s[:, 0:t] + bias_ref[...], s[:, t:]], axis=1) if d < n_diag - 1
                          else s + bias_ref[...])
            shifts.append(query_bias(j)[:, d * t:])
        m = jnp.concatenate(
            [functools.reduce(jnp.maximum, [jnp.max(scores[d][:, (b - d) * t:(b - d + 1) * t], axis=0, keepdims=True)
                                            + shifts[d][:, (b - d) * t:(b - d + 1) * t] for d in range(b + 1)])
             for b in range(n_diag)], axis=1)
        for d, j in enumerate(js):
            p = jnp.exp(scores[d] - (m[:, d * t:] - shifts[d])).astype(BF16)
            pv = jnp.dot(vt_ref[j], p, preferred_element_type=F32)
            if d == 0:
                acc_ref[idx] = pv
            else:
                acc_ref[idx, :, d * t:] += pv
        return m

    def headroom(j_top, m1, m2):
        k_sq = kmax_ref[jnp.maximum(j_top, 0) + 1]
        reach = jnp.sqrt(q_sq * k_sq[:, 0:1])
        nearest = -slope * (q_pos - ((j_top + 1) * t - 1)).astype(F32)
        return jnp.max(jnp.maximum(reach[0:1] - m1, reach[1:2] - m2) + nearest)

    m1 = diagonal(0, q1)
    m2 = diagonal(1, q2)

    def cond(c):
        return jnp.logical_and(c[0] >= 0, c[1] > -EXP_ZERO_BELOW)

    def body(c):
        j_top, room, m1, m2 = c
        j_next = j_top - DIFF_GROUP

        def fast():
            return (headroom(j_next, m1, m2),) + group(j_top, m1, m2, accumulate)

        def exact():
            return (headroom(j_next, m1, m2),) + group(j_top, m1, m2, update)

        room, m1, m2 = lax.cond(room < FAST_BELOW, fast, exact)
        return j_next, room, m1, m2

    j_first = qi * n_diag - 1
    lax.while_loop(cond, body, (j_first, headroom(j_first, m1, m2), m1, m2))

    lam = lam_ref[...]
    lam_full = (jnp.exp(jnp.sum(lam[0:1] * lam[1:2], keepdims=True))
                - jnp.exp(jnp.sum(lam[2:3] * lam[3:4], keepdims=True)) + lam_init)
    a1 = acc_ref[0]
    a2 = acc_ref[1]
    out_t = (a1[0:LANES] / a1[LANES:LANES + 1]
             - lam_full * (a2[0:LANES] / a2[LANES:LANES + 1]))
    out = out_t.T
    ms = jnp.mean(out * out, axis=-1, keepdims=True)
    out = out * lax.rsqrt(ms + RMS_EPS) * g_ref[...] * (1.0 - lam_init)
    o_ref[0] = out.astype(o_ref.dtype)


def _diff_attention(u, d, lam, g_sub, lam_init):
    b, s, _ = u.shape
    t = ATT_TILE
    tq = DIFF_Q_TILE
    n_heads = d // LANES
    return pl.pallas_call(
        functools.partial(_diff_kernel, t=t, tq=tq, n_heads=n_heads, lam_init=lam_init),
        grid=(b, n_heads, s // tq),
        in_specs=[
            pl.BlockSpec((1, tq, LANES), lambda i, h, j: (i, j, h)),
            pl.BlockSpec((1, s, LANES), lambda i, h, j: (i, 0, n_heads + h)),
            pl.BlockSpec((1, s, LANES), lambda i, h, j: (i, 0, 2 * n_heads + h)),
            pl.BlockSpec(lam.shape, lambda i, h, j: (0, 0)),
            pl.BlockSpec((1, LANES), lambda i, h, j: (0, 0)),
        ],
        out_specs=pl.BlockSpec((1, tq, LANES), lambda i, h, j: (i, j, h)),
        out_shape=jax.ShapeDtypeStruct((b, s, d), BF16),
        scratch_shapes=[pltpu.VMEM((s // t, LANES + ROWS_ONES, t), BF16),
                        pltpu.VMEM((s // t + 1, SUBLANES, LANES), F32),
                        pltpu.VMEM((t, t), F32),
                        pltpu.VMEM((2, LANES + ROWS_ONES, tq), F32)],
        compiler_params=pltpu.CompilerParams(
            dimension_semantics=("arbitrary", "arbitrary", "arbitrary"),
            vmem_limit_bytes=VMEM_LIMIT),
        name="diff_attn",
    )(u, u, u, lam, g_sub.reshape(1, LANES))


def kernel(x, c, w_ada, b_ada, g_pre, g_post, w_in, w_out, diff_lambda, diff_subln):
    depth = w_ada.shape[0]
    d = x.shape[-1]
    assert w_in.shape[2] == 4 * d and d % LANES == 0
    assert x.shape[1] % ROW_TILE == 0 and x.shape[1] % DIFF_Q_TILE == 0
    assert DIFF_Q_TILE % ATT_TILE == 0 and ATT_TILE % CHUNK == 0
    ada = _ada(c, w_ada, b_ada)
    w_in_b = w_in.astype(BF16)
    w_out_b = w_out.astype(BF16)
    u = _pre(x, g_pre[0], ada[0], w_in_b[0])
    for i in range(depth):
        if i % N_MIXERS == 0:
            y = _sb_attention(u, d)
        else:
            j = i // N_MIXERS
            y = _diff_attention(u, d, diff_lambda[j], diff_subln[j], _diff_lambda_init(i))
        if i + 1 < depth:
            x, u = _post(y, u, x, g_post[i], ada[i], w_out_b[i], (g_pre[i + 1], ada[i + 1], w_in_b[i + 1]))
        else:
            x = _post(y, u, x, g_post[i], ada[i], w_out_b[i])
    return x
```

```python
import functools
import math

import jax
import jax.numpy as jnp
from jax import lax
from jax.experimental import pallas as pl
from jax.experimental.pallas import tpu as pltpu

F32 = jnp.float32
BF16 = jnp.bfloat16

LANES = 128
SUBLANES = 8
N_MIXERS = 2
CHUNK = 64
SB_HEAD_DIM = 64
DIFF_HEAD_DIM = 64
ALIBI_MAX_EXP = 8.0
RMS_EPS = 1e-6
NEG_INF = -1e30
LOG2E = 1.4426950408889634
SB_LOGIT_CAP = 126.0
SB_EXIT_BITS = 160.0
DIFF_GROUP = 4
EXP_ZERO_BELOW = 104.0
FAST_BELOW = 30.0
BOUND_SLACK = 1.05
ROWS_ONES = 16

ROW_TILE = 512
ATT_TILE = 256
SB_Q_TILES = 2
DIFF_Q_TILE = 1024
VMEM_LIMIT = 48 * 1024 * 1024

_NT = (((1,), (1,)), ((), ()))


def _diff_lambda_init(layer_idx):
    return 0.8 - 0.6 * math.exp(-0.3 * layer_idx)


def _ada_kernel(c_ref, w_ref, b_ref, o_ref):
    c = c_ref[...]
    cond = c * jax.nn.sigmoid(c)
    o_ref[0] = jnp.dot(cond, w_ref[0], preferred_element_type=F32,
                       precision=lax.Precision.HIGHEST) + b_ref[0]


def _ada(c, w_ada, b_ada):
    depth, d, d3 = w_ada.shape
    nb = c.shape[0]
    b = -(-nb // SUBLANES) * SUBLANES
    c = jnp.pad(c, ((0, b - nb), (0, 0)))
    nt = d3 // d
    out = pl.pallas_call(
        _ada_kernel,
        grid=(depth, nt),
        in_specs=[
            pl.BlockSpec((b, d), lambda i, j: (0, 0)),
            pl.BlockSpec((1, d, d), lambda i, j: (i, 0, j)),
            pl.BlockSpec((1, 1, d), lambda i, j: (i, 0, j)),
        ],
        out_specs=pl.BlockSpec((1, b, d), lambda i, j: (i, 0, j)),
        out_shape=jax.ShapeDtypeStruct((depth, b, d3), F32),
        name="ada",
    )(c, w_ada, b_ada.reshape(depth, 1, d3))
    return out[:, :nb]


def _project(x, g_ref, ada_ref, w_ref, u_ref, d):
    ms = jnp.mean(x * x, axis=-1, keepdims=True)
    y = x * lax.rsqrt(ms + RMS_EPS) * g_ref[...]
    ada = ada_ref[0]
    h = (y * (1.0 + ada[:, d:2 * d]) + ada[:, 0:d]).astype(BF16)
    n_out = w_ref.shape[1]
    for n in range(n_out // d):
        u_ref[0, :, n * d:(n + 1) * d] = jnp.dot(
            h, w_ref[:, n * d:(n + 1) * d], preferred_element_type=F32).astype(BF16)


def _residual(y_ref, z_ref, x_ref, g_ref, ada_ref, w_ref, d):
    y = y_ref[0].astype(F32)
    z = z_ref[0].astype(F32)
    t = (y * (z * jax.nn.sigmoid(z))).astype(BF16)
    r = jnp.dot(t, w_ref[...], preferred_element_type=F32)
    ms = jnp.mean(r * r, axis=-1, keepdims=True)
    rn = r * lax.rsqrt(ms + RMS_EPS) * g_ref[...]
    gate = ada_ref[0][:, 2 * d:3 * d]
    return x_ref[0] + gate * rn


def _pre_kernel(x_ref, g_ref, ada_ref, w_ref, u_ref, *, d):
    _project(x_ref[0], g_ref, ada_ref, w_ref, u_ref, d)


def _post_kernel(y_ref, z_ref, x_ref, g_ref, ada_ref, w_ref, o_ref, *, d):
    o_ref[0] = _residual(y_ref, z_ref, x_ref, g_ref, ada_ref, w_ref, d)


def _post_pre_kernel(y_ref, z_ref, x_ref, g_post_ref, ada_ref, w_out_ref, g_pre_ref, ada_next_ref, w_in_ref,
                     o_ref, u_ref, *, d):
    x = _residual(y_ref, z_ref, x_ref, g_post_ref, ada_ref, w_out_ref, d)
    o_ref[0] = x
    _project(x, g_pre_ref, ada_next_ref, w_in_ref, u_ref, d)


def _row_specs(tm, d):
    row = pl.BlockSpec((1, tm, d), lambda i, j: (i, j, 0))
    gain = pl.BlockSpec((1, d), lambda i, j: (0, 0))
    ada = pl.BlockSpec((1, 1, 3 * d), lambda i, j: (i, 0, 0))
    return row, gain, ada


_ROW_PARAMS = dict(compiler_params=pltpu.CompilerParams(
    dimension_semantics=("arbitrary", "arbitrary"), vmem_limit_bytes=VMEM_LIMIT))


def _pre(x, g_pre, ada, w_in):
    b, s, d = x.shape
    n_out = w_in.shape[1]
    tm = ROW_TILE
    row, gain, ada_spec = _row_specs(tm, d)
    return pl.pallas_call(
        functools.partial(_pre_kernel, d=d),
        grid=(b, s // tm),
        in_specs=[row, gain, ada_spec, pl.BlockSpec((d, n_out), lambda i, j: (0, 0))],
        out_specs=pl.BlockSpec((1, tm, n_out), lambda i, j: (i, j, 0)),
        out_shape=jax.ShapeDtypeStruct((b, s, n_out), BF16),
        name="pre", **_ROW_PARAMS,
    )(x, g_pre.reshape(1, d), ada.reshape(b, 1, 3 * d), w_in)


def _post(y, u, x, g_post, ada, w_out, nxt=None):
    b, s, d = x.shape
    tm = ROW_TILE
    z_col = u.shape[2] // d - 1
    row, gain, ada_spec = _row_specs(tm, d)
    in_specs = [row, pl.BlockSpec((1, tm, d), lambda i, j: (i, j, z_col)), row, gain, ada_spec,
                pl.BlockSpec((d, d), lambda i, j: (0, 0))]
    args = [y, u, x, g_post.reshape(1, d), ada.reshape(b, 1, 3 * d), w_out]
    if nxt is None:
        return pl.pallas_call(
            functools.partial(_post_kernel, d=d),
            grid=(b, s // tm), in_specs=in_specs, out_specs=row,
            out_shape=jax.ShapeDtypeStruct((b, s, d), F32),
            name="post", **_ROW_PARAMS,
        )(*args)
    g_pre, ada_next, w_in = nxt
    n_out = w_in.shape[1]
    return pl.pallas_call(
        functools.partial(_post_pre_kernel, d=d),
        grid=(b, s // tm),
        in_specs=in_specs + [gain, ada_spec, pl.BlockSpec((d, n_out), lambda i, j: (0, 0))],
        out_specs=[row, pl.BlockSpec((1, tm, n_out), lambda i, j: (i, j, 0))],
        out_shape=[jax.ShapeDtypeStruct((b, s, d), F32), jax.ShapeDtypeStruct((b, s, n_out), BF16)],
        name="post_pre", **_ROW_PARAMS,
    )(*args, g_pre.reshape(1, d), ada_next.reshape(b, 1, 3 * d), w_in)


def _sb_kernel(q_ref, k_ref, v_ref, o_ref, vt_ref, acc_ref, *, t, n_sub):
    eye = jnp.where(lax.broadcasted_iota(jnp.int32, (LANES, LANES), 0)
                    == lax.broadcasted_iota(jnp.int32, (LANES, LANES), 1), 1.0, 0.0).astype(BF16)
    for sub in range(n_sub):
        j = pl.program_id(2) * n_sub + sub
        v_tile = v_ref[0, pl.ds(pl.multiple_of(j * t, t), t), :]
        vt_ref[j] = lax.dot_general(eye, v_tile, _NT, preferred_element_type=F32).astype(BF16)

    lane = lax.broadcasted_iota(jnp.int32, (t, LANES), 1)
    kpos = lax.broadcasted_iota(jnp.int32, (t, 2 * t), 0)
    qpos = lax.broadcasted_iota(jnp.int32, (t, 2 * t), 1) % t
    strict = kpos < qpos
    upper = jnp.where(lax.broadcasted_iota(jnp.int32, (t, t), 1)
                      > lax.broadcasted_iota(jnp.int32, (t, t), 0), 1.0, 0.0).astype(BF16)

    def logits(qs, j):
        rows = pl.ds(pl.multiple_of(j * t, t), t)
        return lax.dot_general(k_ref[0, rows, :], qs, _NT, preferred_element_type=F32)

    def soften(z, diag):
        z = jnp.minimum(z, SB_LOGIT_CAP)
        drop = jnp.log2(1.0 + jnp.exp2(z))
        log_beta = z - drop
        return (jnp.where(strict, drop, 0.0) if diag else drop), log_beta

    def suffix(drop):
        return jnp.dot(upper, drop.astype(BF16), preferred_element_type=F32)

    def weigh(j, log_beta, between, carry, diag):
        w = jnp.exp2(log_beta - between - carry)
        if diag:
            w = jnp.where(strict, w, 0.0)
        return jnp.dot(vt_ref[j], w.astype(BF16), preferred_element_type=F32)

    def live(carry):
        return jnp.min(carry) < SB_EXIT_BITS

    work = []
    for sub in range(n_sub):
        qt = pl.program_id(2) * n_sub + sub
        q = q_ref[0, sub * t:(sub + 1) * t, :].astype(F32) * ((SB_HEAD_DIM ** -0.5) * LOG2E)
        qs = jnp.concatenate([jnp.where(lane < SB_HEAD_DIM, q, 0.0).astype(BF16),
                              jnp.where(lane < SB_HEAD_DIM, 0.0, q).astype(BF16)], axis=0)
        j_prev = jnp.maximum(qt - 1, 0)
        work.append(dict(sub=sub, qt=qt, qs=qs, j_prev=j_prev,
                         z_d=logits(qs, qt), z_p=logits(qs, j_prev)))
    for w_ in work:
        w_["drop_d"], w_["lb_d"] = soften(w_["z_d"], True)
        w_["btw_d"] = suffix(w_["drop_d"])
        w_["drop_p"], w_["lb_p"] = soften(w_["z_p"], False)
        w_["btw_p"] = suffix(w_["drop_p"])
    for w_ in work:
        carry = w_["btw_d"][0:1, :] + w_["drop_d"][0:1, :]
        has_prev = jnp.full((1, 2 * t), w_["qt"], jnp.int32) >= 1
        carry = jnp.where(has_prev, carry, -NEG_INF)
        pv = (weigh(w_["qt"], w_["lb_d"], w_["btw_d"], jnp.zeros((1, 2 * t), F32), True)
              + weigh(w_["j_prev"], w_["lb_p"], w_["btw_p"], carry, False))
        acc_ref[w_["sub"]] = pv
        w_["carry"] = carry + w_["btw_p"][0:1, :] + w_["drop_p"][0:1, :]

    for w_ in work:
        def cond(c):
            return jnp.logical_and(c[0] >= 0, c[1])

        def body(c, sub=w_["sub"], qs=w_["qs"]):
            j, _, carry = c
            drop, log_beta = soften(logits(qs, j), False)
            between = suffix(drop)
            acc_ref[sub] += weigh(j, log_beta, between, carry, False)
            carry = carry + between[0:1, :] + drop[0:1, :]
            return j - 1, live(carry), carry

        lax.while_loop(cond, body, (w_["qt"] - 2, live(w_["carry"]), w_["carry"]))
        acc = acc_ref[w_["sub"]]
        out_t = jnp.concatenate([acc[0:SB_HEAD_DIM, 0:t], acc[SB_HEAD_DIM:LANES, t:2 * t]], axis=0)
        o_ref[0, w_["sub"] * t:(w_["sub"] + 1) * t, :] = out_t.T.astype(o_ref.dtype)


def _sb_attention(u, d):
    b, s, _ = u.shape
    t = ATT_TILE
    n_sub = SB_Q_TILES
    npair = d // LANES
    return pl.pallas_call(
        functools.partial(_sb_kernel, t=t, n_sub=n_sub),
        grid=(b, npair, s // (t * n_sub)),
        in_specs=[
            pl.BlockSpec((1, t * n_sub, LANES), lambda i, h, j: (i, j, h)),
            pl.BlockSpec((1, s, LANES), lambda i, h, j: (i, 0, npair + h)),
            pl.BlockSpec((1, s, LANES), lambda i, h, j: (i, 0, 2 * npair + h)),
        ],
        out_specs=pl.BlockSpec((1, t * n_sub, LANES), lambda i, h, j: (i, j, h)),
        out_shape=jax.ShapeDtypeStruct((b, s, d), BF16),
        scratch_shapes=[pltpu.VMEM((s // t, LANES, t), BF16),
                        pltpu.VMEM((n_sub, LANES, 2 * t), F32)],
        compiler_params=pltpu.CompilerParams(
            dimension_semantics=("arbitrary", "arbitrary", "arbitrary"),
            vmem_limit_bytes=VMEM_LIMIT),
        name="sb_attn",
    )(u, u, u)


def _diff_kernel(q_ref, k_ref, v_ref, lam_ref, g_ref, o_ref, vt_ref, kmax_ref, bias_ref, acc_ref,
                 *, t, tq, n_heads, lam_init):
    h = pl.program_id(1)
    qi = pl.program_id(2)
    head_no = (lax.broadcasted_iota(jnp.int32, (1, 1), 0) + (h + 1)).astype(F32)
    slope = jnp.exp2(-head_no * (ALIBI_MAX_EXP / n_heads))
    lane = lax.broadcasted_iota(jnp.int32, (tq, LANES), 1)
    n_diag = tq // t
    half = lax.broadcasted_iota(jnp.int32, (SUBLANES, LANES), 0)
    feat = lax.broadcasted_iota(jnp.int32, (SUBLANES, LANES), 1)
    pick = jnp.where((feat >= DIFF_HEAD_DIM) == (half == 1), 1.0, 0.0)
    pick = jnp.where(half < 2, pick, 0.0).astype(BF16)

    def sq_norms(x):
        xf = x.astype(F32)
        return lax.dot_general(pick, (xf * xf).astype(BF16), _NT, preferred_element_type=F32)

    @pl.when(qi == 0)
    def _():
        kmax_ref[0] = jnp.zeros((SUBLANES, LANES), F32)
        kpos = lax.broadcasted_iota(jnp.int32, (t, t), 0)
        qpos = lax.broadcasted_iota(jnp.int32, (t, t), 1)
        ahead = jnp.maximum(kpos - qpos, 0).astype(F32)
        bias_ref[...] = jnp.where(kpos // CHUNK <= qpos // CHUNK, -2.0 * slope * ahead, NEG_INF)

    eye = jnp.where(lax.broadcasted_iota(jnp.int32, (LANES, LANES), 0)
                    == lax.broadcasted_iota(jnp.int32, (LANES, LANES), 1), 1.0, 0.0).astype(BF16)
    for d in range(n_diag):
        j = qi * n_diag + d
        rows = pl.ds(pl.multiple_of(j * t, t), t)
        vt = lax.dot_general(eye, v_ref[0, rows, :], _NT, preferred_element_type=F32)
        vt_ref[j, 0:LANES, :] = vt.astype(BF16)
        vt_ref[j, LANES:LANES + ROWS_ONES, :] = jnp.ones((ROWS_ONES, t), BF16)
        k_sq = jnp.max(sq_norms(k_ref[0, rows, :]), axis=1, keepdims=True)
        kmax_ref[j + 1] = jnp.maximum(kmax_ref[j], k_sq)

    sub = lax.broadcasted_iota(jnp.int32, (t, LANES), 0)
    q_raw = q_ref[0]
    q = q_raw.astype(F32) * (DIFF_HEAD_DIM ** -0.5)
    one_col = jnp.where(lane == 0, 1.0, 0.0).astype(BF16)
    q1 = jnp.concatenate([jnp.where(lane < DIFF_HEAD_DIM, q, 0.0).astype(BF16), one_col], axis=1)
    q2 = jnp.concatenate([jnp.where(lane < DIFF_HEAD_DIM, 0.0, q).astype(BF16), one_col], axis=1)
    k_aug = jnp.where(lax.broadcasted_iota(jnp.int32, (t, LANES), 1) == 0,
                      slope * sub.astype(F32), 0.0).astype(BF16)
    q_pos = lax.broadcasted_iota(jnp.int32, (1, tq), 1) + qi * tq
    q_sq = sq_norms(q_raw) * (BOUND_SLACK / DIFF_HEAD_DIM)

    def key_tile(j):
        k2 = k_ref[0, pl.ds(pl.multiple_of(j * t, t), t), :]
        return jnp.concatenate([k2, k_aug], axis=1)

    def query_bias(j):
        return -slope * (q_pos - j * t).astype(F32)

    def update(idx, scores, biases, vt, m):
        m_new = m
        for s, c in zip(scores, biases):
            m_new = jnp.maximum(m_new, jnp.max(s, axis=0, keepdims=True) + c)
        alpha = jnp.exp(m - m_new)
        p = jnp.concatenate([jnp.exp(s - (m_new - c)).astype(BF16) for s, c in zip(scores, biases)], axis=0)
        acc_ref[idx] = alpha * acc_ref[idx] + jnp.dot(vt, p, preferred_element_type=F32)
        return m_new

    def accumulate(idx, scores, biases, vt, m):
        p = jnp.concatenate([jnp.exp(s - (m - c)).astype(BF16) for s, c in zip(scores, biases)], axis=0)
        acc_ref[idx] += jnp.dot(vt, p, preferred_element_type=F32)
        return m

    def group(j_top, m1, m2, step):
        js = [jnp.maximum(j_top - g, 0) for g in range(DIFF_GROUP)]
        cs = [jnp.where(jnp.full((1, tq), j_top - g, jnp.int32) >= 0, query_bias(js[g]), NEG_INF)
              for g in range(DIFF_GROUP)]
        ks = [key_tile(j) for j in js]
        vt = jnp.concatenate([vt_ref[j] for j in js], axis=1)
        s1 = [lax.dot_general(k, q1, _NT, preferred_element_type=F32) for k in ks]
        s2 = [lax.dot_general(k, q2, _NT, preferred_element_type=F32) for k in ks]
        return step(0, s1, cs, vt, m1), step(1, s2, cs, vt, m2)

    def diagonal(idx, qm):
        js = [qi * n_diag + d for d in range(n_diag)]
        scores, shifts = [], []
        for d, j in enumerate(js):
            s = lax.dot_general(key_tile(j), qm[d * t:, :], _NT, preferred_element_type=F32)
            scores.append(jnp.concatenate([s[:, 0:t] + bias_ref[...], s[:, t:]], axis=1) if d < n_diag - 1
                          else s + bias_ref[...])
            shifts.append(query_bias(j)[:, d * t:])
        m = jnp.concatenate(
            [functools.reduce(jnp.maximum, [jnp.max(scores[d][:, (b - d) * t:(b - d + 1) * t], axis=0, keepdims=True)
                                            + shifts[d][:, (b - d) * t:(b - d + 1) * t] for d in range(b + 1)])
             for b in range(n_diag)], axis=1)
        for d, j in enumerate(js):
            p = jnp.exp(scores[d] - (m[:, d * t:] - shifts[d])).astype(BF16)
            pv = jnp.dot(vt_ref[j], p, preferred_element_type=F32)
            if d == 0:
                acc_ref[idx] = pv
            else:
                acc_ref[idx, :, d * t:] += pv
        return m

    def headroom(j_top, m1, m2):
        k_sq = kmax_ref[jnp.maximum(j_top, 0) + 1]
        reach = jnp.sqrt(q_sq * k_sq[:, 0:1])
        nearest = -slope * (q_pos - ((j_top + 1) * t - 1)).astype(F32)
        return jnp.max(jnp.maximum(reach[0:1] - m1, reach[1:2] - m2) + nearest)

    m1 = diagonal(0, q1)
    m2 = diagonal(1, q2)

    def cond(c):
        return jnp.logical_and(c[0] >= 0, c[1] > -EXP_ZERO_BELOW)

    def body(c):
        j_top, room, m1, m2 = c
        j_next = j_top - DIFF_GROUP

        def fast():
            return (headroom(j_next, m1, m2),) + group(j_top, m1, m2, accumulate)

        def exact():
            return (headroom(j_next, m1, m2),) + group(j_top, m1, m2, update)

        room, m1, m2 = lax.cond(room < FAST_BELOW, fast, exact)
        return j_next, room, m1, m2

    j_first = qi * n_diag - 1
    lax.while_loop(cond, body, (j_first, headroom(j_first, m1, m2), m1, m2))

    lam = lam_ref[...]
    lam_full = (jnp.exp(jnp.sum(lam[0:1] * lam[1:2], keepdims=True))
                - jnp.exp(jnp.sum(lam[2:3] * lam[3:4], keepdims=True)) + lam_init)
    a1 = acc_ref[0]
    a2 = acc_ref[1]
    out_t = (a1[0:LANES] / a1[LANES:LANES + 1]
             - lam_full * (a2[0:LANES] / a2[LANES:LANES + 1]))
    out = out_t.T
    ms = jnp.mean(out * out, axis=-1, keepdims=True)
    out = out * lax.rsqrt(ms + RMS_EPS) * g_ref[...] * (1.0 - lam_init)
    o_ref[0] = out.astype(o_ref.dtype)


def _diff_attention(u, d, lam, g_sub, lam_init):
    b, s, _ = u.shape
    t = ATT_TILE
    tq = DIFF_Q_TILE
    n_heads = d // LANES
    return pl.pallas_call(
        functools.partial(_diff_kernel, t=t, tq=tq, n_heads=n_heads, lam_init=lam_init),
        grid=(b, n_heads, s // tq),
        in_specs=[
            pl.BlockSpec((1, tq, LANES), lambda i, h, j: (i, j, h)),
            pl.BlockSpec((1, s, LANES), lambda i, h, j: (i, 0, n_heads + h)),
            pl.BlockSpec((1, s, LANES), lambda i, h, j: (i, 0, 2 * n_heads + h)),
            pl.BlockSpec(lam.shape, lambda i, h, j: (0, 0)),
            pl.BlockSpec((1, LANES), lambda i, h, j: (0, 0)),
        ],
        out_specs=pl.BlockSpec((1, tq, LANES), lambda i, h, j: (i, j, h)),
        out_shape=jax.ShapeDtypeStruct((b, s, d), BF16),
        scratch_shapes=[pltpu.VMEM((s // t, LANES + ROWS_ONES, t), BF16),
                        pltpu.VMEM((s // t + 1, SUBLANES, LANES), F32),
                        pltpu.VMEM((t, t), F32),
                        pltpu.VMEM((2, LANES + ROWS_ONES, tq), F32)],
        compiler_params=pltpu.CompilerParams(
            dimension_semantics=("arbitrary", "arbitrary", "arbitrary"),
            vmem_limit_bytes=VMEM_LIMIT),
        name="diff_attn",
    )(u, u, u, lam, g_sub.reshape(1, LANES))


def kernel(x, c, w_ada, b_ada, g_pre, g_post, w_in, w_out, diff_lambda, diff_subln):
    depth = w_ada.shape[0]
    d = x.shape[-1]
    assert w_in.shape[2] == 4 * d and d % LANES == 0
    assert x.shape[1] % ROW_TILE == 0 and x.shape[1] % DIFF_Q_TILE == 0 and x.shape[1] % (ATT_TILE * SB_Q_TILES) == 0
    assert DIFF_Q_TILE % ATT_TILE == 0 and ATT_TILE % CHUNK == 0
    ada = _ada(c, w_ada, b_ada)
    w_in_b = w_in.astype(BF16)
    w_out_b = w_out.astype(BF16)
    u = _pre(x, g_pre[0], ada[0], w_in_b[0])
    for i in range(depth):
        if i % N_MIXERS == 0:
            y = _sb_attention(u, d)
        else:
            j = i // N_MIXERS
            y = _diff_attention(u, d, diff_lambda[j], diff_subln[j], _diff_lambda_init(i))
        if i + 1 < depth:
            x, u = _post(y, u, x, g_post[i], ada[i], w_out_b[i], (g_pre[i + 1], ada[i + 1], w_in_b[i + 1]))
        else:
            x = _post(y, u, x, g_post[i], ada[i], w_out_b[i])
    return x
```

```python
import functools
import math

import jax
import jax.numpy as jnp
from jax import lax
from jax.experimental import pallas as pl
from jax.experimental.pallas import tpu as pltpu

F32 = jnp.float32
BF16 = jnp.bfloat16

LANES = 128
SUBLANES = 8
N_MIXERS = 2
CHUNK = 64
SB_HEAD_DIM = 64
DIFF_HEAD_DIM = 64
ALIBI_MAX_EXP = 8.0
RMS_EPS = 1e-6
NEG_INF = -1e30
LOG2E = 1.4426950408889634
SB_LOGIT_CAP = 126.0
SB_EXIT_BITS = 160.0
DIFF_GROUP = 4
EXP_ZERO_BELOW = 104.0
FAST_BELOW = 30.0
BOUND_SLACK = 1.05
ROWS_ONES = 16

ROW_TILE = 512
ATT_TILE = 256
SB_Q_TILES = 2
DIFF_Q_TILE = 1024
VMEM_LIMIT = 48 * 1024 * 1024

_NT = (((1,), (1,)), ((), ()))


def _diff_lambda_init(layer_idx):
    return 0.8 - 0.6 * math.exp(-0.3 * layer_idx)


def _ada_kernel(c_ref, w_ref, b_ref, o_ref):
    c = c_ref[...]
    cond = c * jax.nn.sigmoid(c)
    o_ref[0] = jnp.dot(cond, w_ref[0], preferred_element_type=F32,
                       precision=lax.Precision.HIGHEST) + b_ref[0]


def _ada(c, w_ada, b_ada):
    depth, d, d3 = w_ada.shape
    nb = c.shape[0]
    b = -(-nb // SUBLANES) * SUBLANES
    c = jnp.pad(c, ((0, b - nb), (0, 0)))
    nt = d3 // d
    out = pl.pallas_call(
        _ada_kernel,
        grid=(depth, nt),
        in_specs=[
            pl.BlockSpec((b, d), lambda i, j: (0, 0)),
            pl.BlockSpec((1, d, d), lambda i, j: (i, 0, j)),
            pl.BlockSpec((1, 1, d), lambda i, j: (i, 0, j)),
        ],
        out_specs=pl.BlockSpec((1, b, d), lambda i, j: (i, 0, j)),
        out_shape=jax.ShapeDtypeStruct((depth, b, d3), F32),
        name="ada",
    )(c, w_ada, b_ada.reshape(depth, 1, d3))
    return out[:, :nb]


def _project(x, g_ref, ada_ref, w_ref, u_ref, d):
    ms = jnp.mean(x * x, axis=-1, keepdims=True)
    y = x * lax.rsqrt(ms + RMS_EPS) * g_ref[...]
    ada = ada_ref[0]
    h = (y * (1.0 + ada[:, d:2 * d]) + ada[:, 0:d]).astype(BF16)
    n_out = w_ref.shape[1]
    for n in range(n_out // d):
        u_ref[0, :, n * d:(n + 1) * d] = jnp.dot(
            h, w_ref[:, n * d:(n + 1) * d], preferred_element_type=F32).astype(BF16)


def _residual(y_ref, z_ref, x_ref, g_ref, ada_ref, w_ref, d):
    y = y_ref[0].astype(F32)
    z = z_ref[0].astype(F32)
    t = (y * (z * jax.nn.sigmoid(z))).astype(BF16)
    r = jnp.dot(t, w_ref[...], preferred_element_type=F32)
    ms = jnp.mean(r * r, axis=-1, keepdims=True)
    rn = r * lax.rsqrt(ms + RMS_EPS) * g_ref[...]
    gate = ada_ref[0][:, 2 * d:3 * d]
    return x_ref[0] + gate * rn


def _pre_kernel(x_ref, g_ref, ada_ref, w_ref, u_ref, *, d):
    _project(x_ref[0], g_ref, ada_ref, w_ref, u_ref, d)


def _post_kernel(y_ref, z_ref, x_ref, g_ref, ada_ref, w_ref, o_ref, *, d):
    o_ref[0] = _residual(y_ref, z_ref, x_ref, g_ref, ada_ref, w_ref, d)


def _post_pre_kernel(y_ref, z_ref, x_ref, g_post_ref, ada_ref, w_out_ref, g_pre_ref, ada_next_ref, w_in_ref,
                     o_ref, u_ref, *, d):
    x = _residual(y_ref, z_ref, x_ref, g_post_ref, ada_ref, w_out_ref, d)
    o_ref[0] = x
    _project(x, g_pre_ref, ada_next_ref, w_in_ref, u_ref, d)


def _row_specs(tm, d):
    row = pl.BlockSpec((1, tm, d), lambda i, j: (i, j, 0))
    gain = pl.BlockSpec((1, d), lambda i, j: (0, 0))
    ada = pl.BlockSpec((1, 1, 3 * d), lambda i, j: (i, 0, 0))
    return row, gain, ada


_ROW_PARAMS = dict(compiler_params=pltpu.CompilerParams(
    dimension_semantics=("arbitrary", "arbitrary"), vmem_limit_bytes=VMEM_LIMIT))


def _pre(x, g_pre, ada, w_in):
    b, s, d = x.shape
    n_out = w_in.shape[1]
    tm = ROW_TILE
    row, gain, ada_spec = _row_specs(tm, d)
    return pl.pallas_call(
        functools.partial(_pre_kernel, d=d),
        grid=(b, s // tm),
        in_specs=[row, gain, ada_spec, pl.BlockSpec((d, n_out), lambda i, j: (0, 0))],
        out_specs=pl.BlockSpec((1, tm, n_out), lambda i, j: (i, j, 0)),
        out_shape=jax.ShapeDtypeStruct((b, s, n_out), BF16),
        name="pre", **_ROW_PARAMS,
    )(x, g_pre.reshape(1, d), ada.reshape(b, 1, 3 * d), w_in)


def _post(y, u, x, g_post, ada, w_out, nxt=None):
    b, s, d = x.shape
    tm = ROW_TILE
    z_col = u.shape[2] // d - 1
    row, gain, ada_spec = _row_specs(tm, d)
    in_specs = [row, pl.BlockSpec((1, tm, d), lambda i, j: (i, j, z_col)), row, gain, ada_spec,
                pl.BlockSpec((d, d), lambda i, j: (0, 0))]
    args = [y, u, x, g_post.reshape(1, d), ada.reshape(b, 1, 3 * d), w_out]
    if nxt is None:
        return pl.pallas_call(
            functools.partial(_post_kernel, d=d),
            grid=(b, s // tm), in_specs=in_specs, out_specs=row,
            out_shape=jax.ShapeDtypeStruct((b, s, d), F32),
            name="post", **_ROW_PARAMS,
        )(*args)
    g_pre, ada_next, w_in = nxt
    n_out = w_in.shape[1]
    return pl.pallas_call(
        functools.partial(_post_pre_kernel, d=d),
        grid=(b, s // tm),
        in_specs=in_specs + [gain, ada_spec, pl.BlockSpec((d, n_out), lambda i, j: (0, 0))],
        out_specs=[row, pl.BlockSpec((1, tm, n_out), lambda i, j: (i, j, 0))],
        out_shape=[jax.ShapeDtypeStruct((b, s, d), F32), jax.ShapeDtypeStruct((b, s, n_out), BF16)],
        name="post_pre", **_ROW_PARAMS,
    )(*args, g_pre.reshape(1, d), ada_next.reshape(b, 1, 3 * d), w_in)


def _sb_kernel(q_ref, k_ref, v_ref, o_ref, vt_ref, acc_ref, *, t, n_sub):
    eye = jnp.where(lax.broadcasted_iota(jnp.int32, (LANES, LANES), 0)
                    == lax.broadcasted_iota(jnp.int32, (LANES, LANES), 1), 1.0, 0.0).astype(BF16)
    for sub in range(n_sub):
        j = pl.program_id(2) * n_sub + sub
        v_tile = v_ref[0, pl.ds(pl.multiple_of(j * t, t), t), :]
        vt_ref[j] = lax.dot_general(eye, v_tile, _NT, preferred_element_type=F32).astype(BF16)

    lane = lax.broadcasted_iota(jnp.int32, (t, LANES), 1)
    kpos = lax.broadcasted_iota(jnp.int32, (t, 2 * t), 0)
    qpos = lax.broadcasted_iota(jnp.int32, (t, 2 * t), 1) % t
    strict = kpos < qpos
    upper = jnp.where(lax.broadcasted_iota(jnp.int32, (t, t), 1)
                      > lax.broadcasted_iota(jnp.int32, (t, t), 0), 1.0, 0.0).astype(BF16)

    def logits(qs, j):
        rows = pl.ds(pl.multiple_of(j * t, t), t)
        return lax.dot_general(k_ref[0, rows, :], qs, _NT, preferred_element_type=F32)

    def soften(z, diag):
        z = jnp.minimum(z, SB_LOGIT_CAP)
        drop = jnp.log2(1.0 + jnp.exp2(z))
        log_beta = z - drop
        return (jnp.where(strict, drop, 0.0) if diag else drop), log_beta

    def suffix(drop):
        return jnp.dot(upper, drop.astype(BF16), preferred_element_type=F32)

    def weigh(j, log_beta, between, carry, diag):
        w = jnp.exp2(log_beta - between - carry)
        if diag:
            w = jnp.where(strict, w, 0.0)
        return jnp.dot(vt_ref[j], w.astype(BF16), preferred_element_type=F32)

    def live(carry):
        return jnp.min(carry) < SB_EXIT_BITS

    work = []
    for sub in range(n_sub):
        qt = pl.program_id(2) * n_sub + sub
        q = q_ref[0, sub * t:(sub + 1) * t, :].astype(F32) * ((SB_HEAD_DIM ** -0.5) * LOG2E)
        qs = jnp.concatenate([jnp.where(lane < SB_HEAD_DIM, q, 0.0).astype(BF16),
                              jnp.where(lane < SB_HEAD_DIM, 0.0, q).astype(BF16)], axis=0)
        j_prev = jnp.maximum(qt - 1, 0)
        work.append(dict(sub=sub, qt=qt, qs=qs, j_prev=j_prev,
                         z_d=logits(qs, qt), z_p=logits(qs, j_prev)))
    for w_ in work:
        w_["drop_d"], w_["lb_d"] = soften(w_["z_d"], True)
        w_["btw_d"] = suffix(w_["drop_d"])
        w_["drop_p"], w_["lb_p"] = soften(w_["z_p"], False)
        w_["btw_p"] = suffix(w_["drop_p"])
    for w_ in work:
        carry = w_["btw_d"][0:1, :] + w_["drop_d"][0:1, :]
        has_prev = jnp.full((1, 2 * t), w_["qt"], jnp.int32) >= 1
        carry = jnp.where(has_prev, carry, -NEG_INF)
        pv = (weigh(w_["qt"], w_["lb_d"], w_["btw_d"], jnp.zeros((1, 2 * t), F32), True)
              + weigh(w_["j_prev"], w_["lb_p"], w_["btw_p"], carry, False))
        acc_ref[w_["sub"]] = pv
        w_["carry"] = carry + w_["btw_p"][0:1, :] + w_["drop_p"][0:1, :]

    for w_ in work:
        def cond(c):
            return jnp.logical_and(c[0] >= 0, c[1])

        def body(c, sub=w_["sub"], qs=w_["qs"]):
            j, _, carry = c
            drop, log_beta = soften(logits(qs, j), False)
            between = suffix(drop)
            acc_ref[sub] += weigh(j, log_beta, between, carry, False)
            carry = carry + between[0:1, :] + drop[0:1, :]
            return j - 1, live(carry), carry

        lax.while_loop(cond, body, (w_["qt"] - 2, live(w_["carry"]), w_["carry"]))
        acc = acc_ref[w_["sub"]]
        out_t = jnp.concatenate([acc[0:SB_HEAD_DIM, 0:t], acc[SB_HEAD_DIM:LANES, t:2 * t]], axis=0)
        o_ref[0, w_["sub"] * t:(w_["sub"] + 1) * t, :] = out_t.T.astype(o_ref.dtype)


def _sb_attention(u, d):
    b, s, _ = u.shape
    t = ATT_TILE
    n_sub = SB_Q_TILES
    npair = d // LANES
    return pl.pallas_call(
        functools.partial(_sb_kernel, t=t, n_sub=n_sub),
        grid=(b, npair, s // (t * n_sub)),
        in_specs=[
            pl.BlockSpec((1, t * n_sub, LANES), lambda i, h, j: (i, j, h)),
            pl.BlockSpec((1, s, LANES), lambda i, h, j: (i, 0, npair + h)),
            pl.BlockSpec((1, s, LANES), lambda i, h, j: (i, 0, 2 * npair + h)),
        ],
        out_specs=pl.BlockSpec((1, t * n_sub, LANES), lambda i, h, j: (i, j, h)),
        out_shape=jax.ShapeDtypeStruct((b, s, d), BF16),
        scratch_shapes=[pltpu.VMEM((s // t, LANES, t), BF16),
                        pltpu.VMEM((n_sub, LANES, 2 * t), F32)],
        compiler_params=pltpu.CompilerParams(
            dimension_semantics=("arbitrary", "arbitrary", "arbitrary"),
            vmem_limit_bytes=VMEM_LIMIT),
        name="sb_attn",
    )(u, u, u)


def _diff_kernel(q_ref, k_ref, v_ref, lam_ref, g_ref, o_ref, vt_ref, kmax_ref, bias_ref, acc_ref,
                 *, t, tq, n_heads, lam_init):
    h = pl.program_id(1)
    qi = pl.program_id(2)
    head_no = (lax.broadcasted_iota(jnp.int32, (1, 1), 0) + (h + 1)).astype(F32)
    slope = jnp.exp2(-head_no * (ALIBI_MAX_EXP / n_heads))
    lane = lax.broadcasted_iota(jnp.int32, (tq, LANES), 1)
    n_diag = tq // t
    half = lax.broadcasted_iota(jnp.int32, (SUBLANES, LANES), 0)
    feat = lax.broadcasted_iota(jnp.int32, (SUBLANES, LANES), 1)
    pick = jnp.where((feat >= DIFF_HEAD_DIM) == (half == 1), 1.0, 0.0)
    pick = jnp.where(half < 2, pick, 0.0).astype(BF16)

    def sq_norms(x):
        xf = x.astype(F32)
        return lax.dot_general(pick, (xf * xf).astype(BF16), _NT, preferred_element_type=F32)

    @pl.when(qi == 0)
    def _():
        kmax_ref[0] = jnp.zeros((SUBLANES, LANES), F32)
        kpos = lax.broadcasted_iota(jnp.int32, (t, t), 0)
        qpos = lax.broadcasted_iota(jnp.int32, (t, t), 1)
        ahead = jnp.maximum(kpos - qpos, 0).astype(F32)
        bias_ref[...] = jnp.where(kpos // CHUNK <= qpos // CHUNK, -2.0 * slope * ahead, NEG_INF)

    eye = jnp.where(lax.broadcasted_iota(jnp.int32, (LANES, LANES), 0)
                    == lax.broadcasted_iota(jnp.int32, (LANES, LANES), 1), 1.0, 0.0).astype(BF16)
    for d in range(n_diag):
        j = qi * n_diag + d
        rows = pl.ds(pl.multiple_of(j * t, t), t)
        vt = lax.dot_general(eye, v_ref[0, rows, :], _NT, preferred_element_type=F32)
        vt_ref[j, 0:LANES, :] = vt.astype(BF16)
        vt_ref[j, LANES:LANES + ROWS_ONES, :] = jnp.ones((ROWS_ONES, t), BF16)
        k_sq = jnp.max(sq_norms(k_ref[0, rows, :]), axis=1, keepdims=True)
        kmax_ref[j + 1] = jnp.maximum(kmax_ref[j], k_sq)

    sub = lax.broadcasted_iota(jnp.int32, (t, LANES), 0)
    q_raw = q_ref[0]
    q = q_raw.astype(F32) * (DIFF_HEAD_DIM ** -0.5)
    one_col = jnp.where(lane == 0, 1.0, 0.0).astype(BF16)
    q1 = jnp.concatenate([jnp.where(lane < DIFF_HEAD_DIM, q, 0.0).astype(BF16), one_col], axis=1)
    q2 = jnp.concatenate([jnp.where(lane < DIFF_HEAD_DIM, 0.0, q).astype(BF16), one_col], axis=1)
    k_aug = jnp.where(lax.broadcasted_iota(jnp.int32, (t, LANES), 1) == 0,
                      slope * sub.astype(F32), 0.0).astype(BF16)
    q_pos = lax.broadcasted_iota(jnp.int32, (1, tq), 1) + qi * tq
    q_sq = sq_norms(q_raw) * (BOUND_SLACK / DIFF_HEAD_DIM)

    def key_tile(j):
        k2 = k_ref[0, pl.ds(pl.multiple_of(j * t, t), t), :]
        return jnp.concatenate([k2, k_aug], axis=1)

    def query_bias(j):
        return -slope * (q_pos - j * t).astype(F32)

    def update(idx, scores, biases, vt, m):
        m_new = m
        for s, c in zip(scores, biases):
            m_new = jnp.maximum(m_new, jnp.max(s, axis=0, keepdims=True) + c)
        alpha = jnp.exp(m - m_new)
        p = jnp.concatenate([jnp.exp(s - (m_new - c)).astype(BF16) for s, c in zip(scores, biases)], axis=0)
        acc_ref[idx] = alpha * acc_ref[idx] + jnp.dot(vt, p, preferred_element_type=F32)
        return m_new

    def accumulate(idx, scores, biases, vt, m):
        p = jnp.concatenate([jnp.exp(s - (m - c)).astype(BF16) for s, c in zip(scores, biases)], axis=0)
        acc_ref[idx] += jnp.dot(vt, p, preferred_element_type=F32)
        return m

    def group(j_top, m1, m2, step):
        js = [jnp.maximum(j_top - g, 0) for g in range(DIFF_GROUP)]
        cs = [jnp.where(jnp.full((1, tq), j_top - g, jnp.int32) >= 0, query_bias(js[g]), NEG_INF)
              for g in range(DIFF_GROUP)]
        ks = [key_tile(j) for j in js]
        vt = jnp.concatenate([vt_ref[j] for j in js], axis=1)
        s1 = [lax.dot_general(k, q1, _NT, preferred_element_type=F32) for k in ks]
        s2 = [lax.dot_general(k, q2, _NT, preferred_element_type=F32) for k in ks]
        return step(0, s1, cs, vt, m1), step(1, s2, cs, vt, m2)

    def diagonal_scores(qm):
        scores = []
        for d in range(n_diag):
            s = lax.dot_general(key_tile(qi * n_diag + d), qm[d * t:, :], _NT, preferred_element_type=F32)
            scores.append(jnp.concatenate([s[:, 0:t] + bias_ref[...], s[:, t:]], axis=1) if d < n_diag - 1
                          else s + bias_ref[...])
        return scores

    def diagonal_softmax(idx, scores):
        shifts = [query_bias(qi * n_diag + d)[:, d * t:] for d in range(n_diag)]
        m = jnp.concatenate(
            [functools.reduce(jnp.maximum, [jnp.max(scores[d][:, (b - d) * t:(b - d + 1) * t], axis=0, keepdims=True)
                                            + shifts[d][:, (b - d) * t:(b - d + 1) * t] for d in range(b + 1)])
             for b in range(n_diag)], axis=1)
        for d in range(n_diag):
            p = jnp.exp(scores[d] - (m[:, d * t:] - shifts[d])).astype(BF16)
            pv = jnp.dot(vt_ref[qi * n_diag + d], p, preferred_element_type=F32)
            if d == 0:
                acc_ref[idx] = pv
            else:
                acc_ref[idx, :, d * t:] += pv
        return m

    def headroom(j_top, m1, m2):
        k_sq = kmax_ref[jnp.maximum(j_top, 0) + 1]
        reach = jnp.sqrt(q_sq * k_sq[:, 0:1])
        nearest = -slope * (q_pos - ((j_top + 1) * t - 1)).astype(F32)
        return jnp.max(jnp.maximum(reach[0:1] - m1, reach[1:2] - m2) + nearest)

    diag1, diag2 = diagonal_scores(q1), diagonal_scores(q2)
    m1 = diagonal_softmax(0, diag1)
    m2 = diagonal_softmax(1, diag2)

    def cond(c):
        return jnp.logical_and(c[0] >= 0, c[1] > -EXP_ZERO_BELOW)

    def body(c):
        j_top, room, m1, m2 = c
        j_next = j_top - DIFF_GROUP

        def fast():
            return (headroom(j_next, m1, m2),) + group(j_top, m1, m2, accumulate)

        def exact():
            return (headroom(j_next, m1, m2),) + group(j_top, m1, m2, update)

        room, m1, m2 = lax.cond(room < FAST_BELOW, fast, exact)
        return j_next, room, m1, m2

    j_first = qi * n_diag - 1
    lax.while_loop(cond, body, (j_first, headroom(j_first, m1, m2), m1, m2))

    lam = lam_ref[...]
    lam_full = (jnp.exp(jnp.sum(lam[0:1] * lam[1:2], keepdims=True))
                - jnp.exp(jnp.sum(lam[2:3] * lam[3:4], keepdims=True)) + lam_init)
    a1 = acc_ref[0]
    a2 = acc_ref[1]
    out_t = (a1[0:LANES] / a1[LANES:LANES + 1]
             - lam_full * (a2[0:LANES] / a2[LANES:LANES + 1]))
    out = out_t.T
    ms = jnp.mean(out * out, axis=-1, keepdims=True)
    out = out * lax.rsqrt(ms + RMS_EPS) * g_ref[...] * (1.0 - lam_init)
    o_ref[0] = out.astype(o_ref.dtype)


def _diff_attention(u, d, lam, g_sub, lam_init):
    b, s, _ = u.shape
    t = ATT_TILE
    tq = DIFF_Q_TILE
    n_heads = d // LANES
    return pl.pallas_call(
        functools.partial(_diff_kernel, t=t, tq=tq, n_heads=n_heads, lam_init=lam_init),
        grid=(b, n_heads, s // tq),
        in_specs=[
            pl.BlockSpec((1, tq, LANES), lambda i, h, j: (i, j, h)),
            pl.BlockSpec((1, s, LANES), lambda i, h, j: (i, 0, n_heads + h)),
            pl.BlockSpec((1, s, LANES), lambda i, h, j: (i, 0, 2 * n_heads + h)),
            pl.BlockSpec(lam.shape, lambda i, h, j: (0, 0)),
            pl.BlockSpec((1, LANES), lambda i, h, j: (0, 0)),
        ],
        out_specs=pl.BlockSpec((1, tq, LANES), lambda i, h, j: (i, j, h)),
        out_shape=jax.ShapeDtypeStruct((b, s, d), BF16),
        scratch_shapes=[pltpu.VMEM((s // t, LANES + ROWS_ONES, t), BF16),
                        pltpu.VMEM((s // t + 1, SUBLANES, LANES), F32),
                        pltpu.VMEM((t, t), F32),
                        pltpu.VMEM((2, LANES + ROWS_ONES, tq), F32)],
        compiler_params=pltpu.CompilerParams(
            dimension_semantics=("arbitrary", "arbitrary", "arbitrary"),
            vmem_limit_bytes=VMEM_LIMIT),
        name="diff_attn",
    )(u, u, u, lam, g_sub.reshape(1, LANES))


def kernel(x, c, w_ada, b_ada, g_pre, g_post, w_in, w_out, diff_lambda, diff_subln):
    depth = w_ada.shape[0]
    d = x.shape[-1]
    assert w_in.shape[2] == 4 * d and d % LANES == 0
    assert x.shape[1] % ROW_TILE == 0 and x.shape[1] % DIFF_Q_TILE == 0 and x.shape[1] % (ATT_TILE * SB_Q_TILES) == 0
    assert DIFF_Q_TILE % ATT_TILE == 0 and ATT_TILE % CHUNK == 0
    ada = _ada(c, w_ada, b_ada)
    w_in_b = w_in.astype(BF16)
    w_out_b = w_out.astype(BF16)
    u = _pre(x, g_pre[0], ada[0], w_in_b[0])
    for i in range(depth):
        if i % N_MIXERS == 0:
            y = _sb_attention(u, d)
        else:
            j = i // N_MIXERS
            y = _diff_attention(u, d, diff_lambda[j], diff_subln[j], _diff_lambda_init(i))
        if i + 1 < depth:
            x, u = _post(y, u, x, g_post[i], ada[i], w_out_b[i], (g_pre[i + 1], ada[i + 1], w_in_b[i + 1]))
        else:
            x = _post(y, u, x, g_post[i], ada[i], w_out_b[i])
    return x
```

```python
import functools
import math

import jax
import jax.numpy as jnp
from jax import lax
from jax.experimental import pallas as pl
from jax.experimental.pallas import tpu as pltpu

F32 = jnp.float32
BF16 = jnp.bfloat16

LANES = 128
SUBLANES = 8
N_MIXERS = 2
CHUNK = 64
SB_HEAD_DIM = 64
DIFF_HEAD_DIM = 64
ALIBI_MAX_EXP = 8.0
RMS_EPS = 1e-6
NEG_INF = -1e30
LOG2E = 1.4426950408889634
SB_LOGIT_CAP = 126.0
SB_EXIT_BITS = 160.0
DIFF_GROUP = 4
EXP_ZERO_BELOW = 104.0
FAST_BELOW = 30.0
BOUND_SLACK = 1.05
ROWS_ONES = 16

ROW_TILE = 512
ATT_TILE = 256
SB_Q_TILES = 4
DIFF_Q_TILE = 1024
VMEM_LIMIT = 48 * 1024 * 1024

_NT = (((1,), (1,)), ((), ()))


def _diff_lambda_init(layer_idx):
    return 0.8 - 0.6 * math.exp(-0.3 * layer_idx)


def _ada_kernel(c_ref, w_ref, b_ref, o_ref):
    c = c_ref[...]
    cond = c * jax.nn.sigmoid(c)
    o_ref[0] = jnp.dot(cond, w_ref[0], preferred_element_type=F32,
                       precision=lax.Precision.HIGHEST) + b_ref[0]


def _ada(c, w_ada, b_ada):
    depth, d, d3 = w_ada.shape
    nb = c.shape[0]
    b = -(-nb // SUBLANES) * SUBLANES
    c = jnp.pad(c, ((0, b - nb), (0, 0)))
    nt = d3 // d
    out = pl.pallas_call(
        _ada_kernel,
        grid=(depth, nt),
        in_specs=[
            pl.BlockSpec((b, d), lambda i, j: (0, 0)),
            pl.BlockSpec((1, d, d), lambda i, j: (i, 0, j)),
            pl.BlockSpec((1, 1, d), lambda i, j: (i, 0, j)),
        ],
        out_specs=pl.BlockSpec((1, b, d), lambda i, j: (i, 0, j)),
        out_shape=jax.ShapeDtypeStruct((depth, b, d3), F32),
        name="ada",
    )(c, w_ada, b_ada.reshape(depth, 1, d3))
    return out[:, :nb]


def _project(x, g_ref, ada_ref, w_ref, u_ref, d):
    ms = jnp.mean(x * x, axis=-1, keepdims=True)
    y = x * lax.rsqrt(ms + RMS_EPS) * g_ref[...]
    ada = ada_ref[0]
    h = (y * (1.0 + ada[:, d:2 * d]) + ada[:, 0:d]).astype(BF16)
    n_out = w_ref.shape[1]
    for n in range(n_out // d):
        u_ref[0, :, n * d:(n + 1) * d] = jnp.dot(
            h, w_ref[:, n * d:(n + 1) * d], preferred_element_type=F32).astype(BF16)


def _residual(y_ref, z_ref, x_ref, g_ref, ada_ref, w_ref, d):
    y = y_ref[0].astype(F32)
    z = z_ref[0].astype(F32)
    t = (y * (z * jax.nn.sigmoid(z))).astype(BF16)
    r = jnp.dot(t, w_ref[...], preferred_element_type=F32)
    ms = jnp.mean(r * r, axis=-1, keepdims=True)
    rn = r * lax.rsqrt(ms + RMS_EPS) * g_ref[...]
    gate = ada_ref[0][:, 2 * d:3 * d]
    return x_ref[0] + gate * rn


def _pre_kernel(x_ref, g_ref, ada_ref, w_ref, u_ref, *, d):
    _project(x_ref[0], g_ref, ada_ref, w_ref, u_ref, d)


def _post_kernel(y_ref, z_ref, x_ref, g_ref, ada_ref, w_ref, o_ref, *, d):
    o_ref[0] = _residual(y_ref, z_ref, x_ref, g_ref, ada_ref, w_ref, d)


def _post_pre_kernel(y_ref, z_ref, x_ref, g_post_ref, ada_ref, w_out_ref, g_pre_ref, ada_next_ref, w_in_ref,
                     o_ref, u_ref, *, d):
    x = _residual(y_ref, z_ref, x_ref, g_post_ref, ada_ref, w_out_ref, d)
    o_ref[0] = x
    _project(x, g_pre_ref, ada_next_ref, w_in_ref, u_ref, d)


def _row_specs(tm, d):
    row = pl.BlockSpec((1, tm, d), lambda i, j: (i, j, 0))
    gain = pl.BlockSpec((1, d), lambda i, j: (0, 0))
    ada = pl.BlockSpec((1, 1, 3 * d), lambda i, j: (i, 0, 0))
    return row, gain, ada


_ROW_PARAMS = dict(compiler_params=pltpu.CompilerParams(
    dimension_semantics=("arbitrary", "arbitrary"), vmem_limit_bytes=VMEM_LIMIT))


def _pre(x, g_pre, ada, w_in):
    b, s, d = x.shape
    n_out = w_in.shape[1]
    tm = ROW_TILE
    row, gain, ada_spec = _row_specs(tm, d)
    return pl.pallas_call(
        functools.partial(_pre_kernel, d=d),
        grid=(b, s // tm),
        in_specs=[row, gain, ada_spec, pl.BlockSpec((d, n_out), lambda i, j: (0, 0))],
        out_specs=pl.BlockSpec((1, tm, n_out), lambda i, j: (i, j, 0)),
        out_shape=jax.ShapeDtypeStruct((b, s, n_out), BF16),
        name="pre", **_ROW_PARAMS,
    )(x, g_pre.reshape(1, d), ada.reshape(b, 1, 3 * d), w_in)


def _post(y, u, x, g_post, ada, w_out, nxt=None):
    b, s, d = x.shape
    tm = ROW_TILE
    z_col = u.shape[2] // d - 1
    row, gain, ada_spec = _row_specs(tm, d)
    in_specs = [row, pl.BlockSpec((1, tm, d), lambda i, j: (i, j, z_col)), row, gain, ada_spec,
                pl.BlockSpec((d, d), lambda i, j: (0, 0))]
    args = [y, u, x, g_post.reshape(1, d), ada.reshape(b, 1, 3 * d), w_out]
    if nxt is None:
        return pl.pallas_call(
            functools.partial(_post_kernel, d=d),
            grid=(b, s // tm), in_specs=in_specs, out_specs=row,
            out_shape=jax.ShapeDtypeStruct((b, s, d), F32),
            name="post", **_ROW_PARAMS,
        )(*args)
    g_pre, ada_next, w_in = nxt
    n_out = w_in.shape[1]
    return pl.pallas_call(
        functools.partial(_post_pre_kernel, d=d),
        grid=(b, s // tm),
        in_specs=in_specs + [gain, ada_spec, pl.BlockSpec((d, n_out), lambda i, j: (0, 0))],
        out_specs=[row, pl.BlockSpec((1, tm, n_out), lambda i, j: (i, j, 0))],
        out_shape=[jax.ShapeDtypeStruct((b, s, d), F32), jax.ShapeDtypeStruct((b, s, n_out), BF16)],
        name="post_pre", **_ROW_PARAMS,
    )(*args, g_pre.reshape(1, d), ada_next.reshape(b, 1, 3 * d), w_in)


def _sb_kernel(q_ref, k_ref, v_ref, o_ref, vt_ref, acc_ref, *, t, n_sub):
    eye = jnp.where(lax.broadcasted_iota(jnp.int32, (LANES, LANES), 0)
                    == lax.broadcasted_iota(jnp.int32, (LANES, LANES), 1), 1.0, 0.0).astype(BF16)
    for sub in range(n_sub):
        j = pl.program_id(2) * n_sub + sub
        v_tile = v_ref[0, pl.ds(pl.multiple_of(j * t, t), t), :]
        vt_ref[j] = lax.dot_general(eye, v_tile, _NT, preferred_element_type=F32).astype(BF16)

    lane = lax.broadcasted_iota(jnp.int32, (t, LANES), 1)
    kpos = lax.broadcasted_iota(jnp.int32, (t, 2 * t), 0)
    qpos = lax.broadcasted_iota(jnp.int32, (t, 2 * t), 1) % t
    strict = kpos < qpos
    upper = jnp.where(lax.broadcasted_iota(jnp.int32, (t, t), 1)
                      > lax.broadcasted_iota(jnp.int32, (t, t), 0), 1.0, 0.0).astype(BF16)

    def logits(qs, j):
        rows = pl.ds(pl.multiple_of(j * t, t), t)
        return lax.dot_general(k_ref[0, rows, :], qs, _NT, preferred_element_type=F32)

    def soften(z, diag):
        z = jnp.minimum(z, SB_LOGIT_CAP)
        drop = jnp.log2(1.0 + jnp.exp2(z))
        log_beta = z - drop
        return (jnp.where(strict, drop, 0.0) if diag else drop), log_beta

    def suffix(drop):
        return jnp.dot(upper, drop.astype(BF16), preferred_element_type=F32)

    def weigh(j, log_beta, between, carry, diag):
        w = jnp.exp2(log_beta - between - carry)
        if diag:
            w = jnp.where(strict, w, 0.0)
        return jnp.dot(vt_ref[j], w.astype(BF16), preferred_element_type=F32)

    def live(carry):
        return jnp.min(carry) < SB_EXIT_BITS

    work = []
    for sub in range(n_sub):
        qt = pl.program_id(2) * n_sub + sub
        q = q_ref[0, sub * t:(sub + 1) * t, :].astype(F32) * ((SB_HEAD_DIM ** -0.5) * LOG2E)
        qs = jnp.concatenate([jnp.where(lane < SB_HEAD_DIM, q, 0.0).astype(BF16),
                              jnp.where(lane < SB_HEAD_DIM, 0.0, q).astype(BF16)], axis=0)
        j_prev = jnp.maximum(qt - 1, 0)
        work.append(dict(sub=sub, qt=qt, qs=qs, j_prev=j_prev,
                         z_d=logits(qs, qt), z_p=logits(qs, j_prev)))
    for w_ in work:
        w_["drop_d"], w_["lb_d"] = soften(w_["z_d"], True)
        w_["btw_d"] = suffix(w_["drop_d"])
        w_["drop_p"], w_["lb_p"] = soften(w_["z_p"], False)
        w_["btw_p"] = suffix(w_["drop_p"])
    for w_ in work:
        carry = w_["btw_d"][0:1, :] + w_["drop_d"][0:1, :]
        has_prev = jnp.full((1, 2 * t), w_["qt"], jnp.int32) >= 1
        carry = jnp.where(has_prev, carry, -NEG_INF)
        pv = (weigh(w_["qt"], w_["lb_d"], w_["btw_d"], jnp.zeros((1, 2 * t), F32), True)
              + weigh(w_["j_prev"], w_["lb_p"], w_["btw_p"], carry, False))
        acc_ref[w_["sub"]] = pv
        w_["carry"] = carry + w_["btw_p"][0:1, :] + w_["drop_p"][0:1, :]

    for w_ in work:
        def cond(c):
            return jnp.logical_and(c[0] >= 0, c[1])

        def body(c, sub=w_["sub"], qs=w_["qs"]):
            j, _, carry = c
            drop, log_beta = soften(logits(qs, j), False)
            between = suffix(drop)
            acc_ref[sub] += weigh(j, log_beta, between, carry, False)
            carry = carry + between[0:1, :] + drop[0:1, :]
            return j - 1, live(carry), carry

        lax.while_loop(cond, body, (w_["qt"] - 2, live(w_["carry"]), w_["carry"]))
        acc = acc_ref[w_["sub"]]
        out_t = jnp.concatenate([acc[0:SB_HEAD_DIM, 0:t], acc[SB_HEAD_DIM:LANES, t:2 * t]], axis=0)
        o_ref[0, w_["sub"] * t:(w_["sub"] + 1) * t, :] = out_t.T.astype(o_ref.dtype)


def _sb_attention(u, d):
    b, s, _ = u.shape
    t = ATT_TILE
    n_sub = SB_Q_TILES
    npair = d // LANES
    return pl.pallas_call(
        functools.partial(_sb_kernel, t=t, n_sub=n_sub),
        grid=(b, npair, s // (t * n_sub)),
        in_specs=[
            pl.BlockSpec((1, t * n_sub, LANES), lambda i, h, j: (i, j, h)),
            pl.BlockSpec((1, s, LANES), lambda i, h, j: (i, 0, npair + h)),
            pl.BlockSpec((1, s, LANES), lambda i, h, j: (i, 0, 2 * npair + h)),
        ],
        out_specs=pl.BlockSpec((1, t * n_sub, LANES), lambda i, h, j: (i, j, h)),
        out_shape=jax.ShapeDtypeStruct((b, s, d), BF16),
        scratch_shapes=[pltpu.VMEM((s // t, LANES, t), BF16),
                        pltpu.VMEM((n_sub, LANES, 2 * t), F32)],
        compiler_params=pltpu.CompilerParams(
            dimension_semantics=("arbitrary", "arbitrary", "arbitrary"),
            vmem_limit_bytes=VMEM_LIMIT),
        name="sb_attn",
    )(u, u, u)


def _diff_kernel(q_ref, k_ref, v_ref, lam_ref, g_ref, o_ref, vt_ref, kmax_ref, bias_ref, acc_ref,
                 *, t, tq, n_heads, lam_init):
    h = pl.program_id(1)
    qi = pl.program_id(2)
    head_no = (lax.broadcasted_iota(jnp.int32, (1, 1), 0) + (h + 1)).astype(F32)
    slope = jnp.exp2(-head_no * (ALIBI_MAX_EXP / n_heads))
    lane = lax.broadcasted_iota(jnp.int32, (tq, LANES), 1)
    n_diag = tq // t
    half = lax.broadcasted_iota(jnp.int32, (SUBLANES, LANES), 0)
    feat = lax.broadcasted_iota(jnp.int32, (SUBLANES, LANES), 1)
    pick = jnp.where((feat >= DIFF_HEAD_DIM) == (half == 1), 1.0, 0.0)
    pick = jnp.where(half < 2, pick, 0.0).astype(BF16)

    def sq_norms(x):
        xf = x.astype(F32)
        return lax.dot_general(pick, (xf * xf).astype(BF16), _NT, preferred_element_type=F32)

    @pl.when(qi == 0)
    def _():
        kmax_ref[0] = jnp.zeros((SUBLANES, LANES), F32)
        kpos = lax.broadcasted_iota(jnp.int32, (t, t), 0)
        qpos = lax.broadcasted_iota(jnp.int32, (t, t), 1)
        ahead = jnp.maximum(kpos - qpos, 0).astype(F32)
        bias_ref[...] = jnp.where(kpos // CHUNK <= qpos // CHUNK, -2.0 * slope * ahead, NEG_INF)

    eye = jnp.where(lax.broadcasted_iota(jnp.int32, (LANES, LANES), 0)
                    == lax.broadcasted_iota(jnp.int32, (LANES, LANES), 1), 1.0, 0.0).astype(BF16)
    for d in range(n_diag):
        j = qi * n_diag + d
        rows = pl.ds(pl.multiple_of(j * t, t), t)
        vt = lax.dot_general(eye, v_ref[0, rows, :], _NT, preferred_element_type=F32)
        vt_ref[j, 0:LANES, :] = vt.astype(BF16)
        vt_ref[j, LANES:LANES + ROWS_ONES, :] = jnp.ones((ROWS_ONES, t), BF16)
        k_sq = jnp.max(sq_norms(k_ref[0, rows, :]), axis=1, keepdims=True)
        kmax_ref[j + 1] = jnp.maximum(kmax_ref[j], k_sq)

    sub = lax.broadcasted_iota(jnp.int32, (t, LANES), 0)
    q_raw = q_ref[0]
    q = q_raw.astype(F32) * (DIFF_HEAD_DIM ** -0.5)
    one_col = jnp.where(lane == 0, 1.0, 0.0).astype(BF16)
    q1 = jnp.concatenate([jnp.where(lane < DIFF_HEAD_DIM, q, 0.0).astype(BF16), one_col], axis=1)
    q2 = jnp.concatenate([jnp.where(lane < DIFF_HEAD_DIM, 0.0, q).astype(BF16), one_col], axis=1)
    k_aug = jnp.where(lax.broadcasted_iota(jnp.int32, (t, LANES), 1) == 0,
                      slope * sub.astype(F32), 0.0).astype(BF16)
    q_pos = lax.broadcasted_iota(jnp.int32, (1, tq), 1) + qi * tq
    q_sq = sq_norms(q_raw) * (BOUND_SLACK / DIFF_HEAD_DIM)

    def key_tile(j):
        k2 = k_ref[0, pl.ds(pl.multiple_of(j * t, t), t), :]
        return jnp.concatenate([k2, k_aug], axis=1)

    def query_bias(j):
        return -slope * (q_pos - j * t).astype(F32)

    def update(idx, scores, biases, vt, m):
        m_new = m
        for s, c in zip(scores, biases):
            m_new = jnp.maximum(m_new, jnp.max(s, axis=0, keepdims=True) + c)
        alpha = jnp.exp(m - m_new)
        p = jnp.concatenate([jnp.exp(s - (m_new - c)).astype(BF16) for s, c in zip(scores, biases)], axis=0)
        acc_ref[idx] = alpha * acc_ref[idx] + jnp.dot(vt, p, preferred_element_type=F32)
        return m_new

    def accumulate(idx, scores, biases, vt, m):
        p = jnp.concatenate([jnp.exp(s - (m - c)).astype(BF16) for s, c in zip(scores, biases)], axis=0)
        acc_ref[idx] += jnp.dot(vt, p, preferred_element_type=F32)
        return m

    def group(j_top, m1, m2, step):
        js = [jnp.maximum(j_top - g, 0) for g in range(DIFF_GROUP)]
        cs = [jnp.where(jnp.full((1, tq), j_top - g, jnp.int32) >= 0, query_bias(js[g]), NEG_INF)
              for g in range(DIFF_GROUP)]
        ks = [key_tile(j) for j in js]
        vt = jnp.concatenate([vt_ref[j] for j in js], axis=1)
        s1 = [lax.dot_general(k, q1, _NT, preferred_element_type=F32) for k in ks]
        s2 = [lax.dot_general(k, q2, _NT, preferred_element_type=F32) for k in ks]
        return step(0, s1, cs, vt, m1), step(1, s2, cs, vt, m2)

    def diagonal_scores(qm):
        scores = []
        for d in range(n_diag):
            s = lax.dot_general(key_tile(qi * n_diag + d), qm[d * t:, :], _NT, preferred_element_type=F32)
            scores.append(jnp.concatenate([s[:, 0:t] + bias_ref[...], s[:, t:]], axis=1) if d < n_diag - 1
                          else s + bias_ref[...])
        return scores

    def diagonal_softmax(idx, scores):
        shifts = [query_bias(qi * n_diag + d)[:, d * t:] for d in range(n_diag)]
        m = jnp.concatenate(
            [functools.reduce(jnp.maximum, [jnp.max(scores[d][:, (b - d) * t:(b - d + 1) * t], axis=0, keepdims=True)
                                            + shifts[d][:, (b - d) * t:(b - d + 1) * t] for d in range(b + 1)])
             for b in range(n_diag)], axis=1)
        for d in range(n_diag):
            p = jnp.exp(scores[d] - (m[:, d * t:] - shifts[d])).astype(BF16)
            pv = jnp.dot(vt_ref[qi * n_diag + d], p, preferred_element_type=F32)
            if d == 0:
                acc_ref[idx] = pv
            else:
                acc_ref[idx, :, d * t:] += pv
        return m

    def headroom(j_top, m1, m2):
        k_sq = kmax_ref[jnp.maximum(j_top, 0) + 1]
        reach = jnp.sqrt(q_sq * k_sq[:, 0:1])
        nearest = -slope * (q_pos - ((j_top + 1) * t - 1)).astype(F32)
        return jnp.max(jnp.maximum(reach[0:1] - m1, reach[1:2] - m2) + nearest)

    diag1, diag2 = diagonal_scores(q1), diagonal_scores(q2)
    m1 = diagonal_softmax(0, diag1)
    m2 = diagonal_softmax(1, diag2)

    def cond(c):
        return jnp.logical_and(c[0] >= 0, c[1] > -EXP_ZERO_BELOW)

    def body(c):
        j_top, room, m1, m2 = c
        j_next = j_top - DIFF_GROUP

        def fast():
            return (headroom(j_next, m1, m2),) + group(j_top, m1, m2, accumulate)

        def exact():
            return (headroom(j_next, m1, m2),) + group(j_top, m1, m2, update)

        room, m1, m2 = lax.cond(room < FAST_BELOW, fast, exact)
        return j_next, room, m1, m2

    j_first = qi * n_diag - 1
    lax.while_loop(cond, body, (j_first, headroom(j_first, m1, m2), m1, m2))

    lam = lam_ref[...]
    lam_full = (jnp.exp(jnp.sum(lam[0:1] * lam[1:2], keepdims=True))
                - jnp.exp(jnp.sum(lam[2:3] * lam[3:4], keepdims=True)) + lam_init)
    a1 = acc_ref[0]
    a2 = acc_ref[1]
    out_t = (a1[0:LANES] / a1[LANES:LANES + 1]
             - lam_full * (a2[0:LANES] / a2[LANES:LANES + 1]))
    out = out_t.T
    ms = jnp.mean(out * out, axis=-1, keepdims=True)
    out = out * lax.rsqrt(ms + RMS_EPS) * g_ref[...] * (1.0 - lam_init)
    o_ref[0] = out.astype(o_ref.dtype)


def _diff_attention(u, d, lam, g_sub, lam_init):
    b, s, _ = u.shape
    t = ATT_TILE
    tq = DIFF_Q_TILE
    n_heads = d // LANES
    return pl.pallas_call(
        functools.partial(_diff_kernel, t=t, tq=tq, n_heads=n_heads, lam_init=lam_init),
        grid=(b, n_heads, s // tq),
        in_specs=[
            pl.BlockSpec((1, tq, LANES), lambda i, h, j: (i, j, h)),
            pl.BlockSpec((1, s, LANES), lambda i, h, j: (i, 0, n_heads + h)),
            pl.BlockSpec((1, s, LANES), lambda i, h, j: (i, 0, 2 * n_heads + h)),
            pl.BlockSpec(lam.shape, lambda i, h, j: (0, 0)),
            pl.BlockSpec((1, LANES), lambda i, h, j: (0, 0)),
        ],
        out_specs=pl.BlockSpec((1, tq, LANES), lambda i, h, j: (i, j, h)),
        out_shape=jax.ShapeDtypeStruct((b, s, d), BF16),
        scratch_shapes=[pltpu.VMEM((s // t, LANES + ROWS_ONES, t), BF16),
                        pltpu.VMEM((s // t + 1, SUBLANES, LANES), F32),
                        pltpu.VMEM((t, t), F32),
                        pltpu.VMEM((2, LANES + ROWS_ONES, tq), F32)],
        compiler_params=pltpu.CompilerParams(
            dimension_semantics=("arbitrary", "arbitrary", "arbitrary"),
            vmem_limit_bytes=VMEM_LIMIT),
        name="diff_attn",
    )(u, u, u, lam, g_sub.reshape(1, LANES))


def kernel(x, c, w_ada, b_ada, g_pre, g_post, w_in, w_out, diff_lambda, diff_subln):
    depth = w_ada.shape[0]
    d = x.shape[-1]
    assert w_in.shape[2] == 4 * d and d % LANES == 0
    assert x.shape[1] % ROW_TILE == 0 and x.shape[1] % DIFF_Q_TILE == 0 and x.shape[1] % (ATT_TILE * SB_Q_TILES) == 0
    assert DIFF_Q_TILE % ATT_TILE == 0 and ATT_TILE % CHUNK == 0
    ada = _ada(c, w_ada, b_ada)
    w_in_b = w_in.astype(BF16)
    w_out_b = w_out.astype(BF16)
    u = _pre(x, g_pre[0], ada[0], w_in_b[0])
    for i in range(depth):
        if i % N_MIXERS == 0:
            y = _sb_attention(u, d)
        else:
            j = i // N_MIXERS
            y = _diff_attention(u, d, diff_lambda[j], diff_subln[j], _diff_lambda_init(i))
        if i + 1 < depth:
            x, u = _post(y, u, x, g_post[i], ada[i], w_out_b[i], (g_pre[i + 1], ada[i + 1], w_in_b[i + 1]))
        else:
            x = _post(y, u, x, g_post[i], ada[i], w_out_b[i])
    return x
```

```python
import functools
import math

import jax
import jax.numpy as jnp
from jax import lax
from jax.experimental import pallas as pl
from jax.experimental.pallas import tpu as pltpu

F32 = jnp.float32
BF16 = jnp.bfloat16

LANES = 128
SUBLANES = 8
N_MIXERS = 2
CHUNK = 64
SB_HEAD_DIM = 64
DIFF_HEAD_DIM = 64
ALIBI_MAX_EXP = 8.0
RMS_EPS = 1e-6
NEG_INF = -1e30
LOG2E = 1.4426950408889634
SB_LOGIT_CAP = 126.0
SB_EXIT_BITS = 160.0
DIFF_GROUP = 4
EXP_ZERO_BELOW = 104.0
FAST_BELOW = 30.0
BOUND_SLACK = 1.05
ROWS_ONES = 16

ROW_TILE = 512
ATT_TILE = 256
SB_Q_TILES = 2
DIFF_Q_TILE = 1024
VMEM_LIMIT = 48 * 1024 * 1024

_NT = (((1,), (1,)), ((), ()))


def _diff_lambda_init(layer_idx):
    return 0.8 - 0.6 * math.exp(-0.3 * layer_idx)


def _ada_kernel(c_ref, w_ref, b_ref, o_ref):
    c = c_ref[...]
    cond = c * jax.nn.sigmoid(c)
    o_ref[0] = jnp.dot(cond, w_ref[0], preferred_element_type=F32,
                       precision=lax.Precision.HIGHEST) + b_ref[0]


def _ada(c, w_ada, b_ada):
    depth, d, d3 = w_ada.shape
    nb = c.shape[0]
    b = -(-nb // SUBLANES) * SUBLANES
    c = jnp.pad(c, ((0, b - nb), (0, 0)))
    nt = d3 // d
    out = pl.pallas_call(
        _ada_kernel,
        grid=(depth, nt),
        in_specs=[
            pl.BlockSpec((b, d), lambda i, j: (0, 0)),
            pl.BlockSpec((1, d, d), lambda i, j: (i, 0, j)),
            pl.BlockSpec((1, 1, d), lambda i, j: (i, 0, j)),
        ],
        out_specs=pl.BlockSpec((1, b, d), lambda i, j: (i, 0, j)),
        out_shape=jax.ShapeDtypeStruct((depth, b, d3), F32),
        name="ada",
    )(c, w_ada, b_ada.reshape(depth, 1, d3))
    return out[:, :nb]


def _project(x, g_ref, ada_ref, w_ref, u_ref, d):
    ms = jnp.mean(x * x, axis=-1, keepdims=True)
    y = x * lax.rsqrt(ms + RMS_EPS) * g_ref[...]
    ada = ada_ref[0]
    h = (y * (1.0 + ada[:, d:2 * d]) + ada[:, 0:d]).astype(BF16)
    n_out = w_ref.shape[1]
    for n in range(n_out // d):
        u_ref[0, :, n * d:(n + 1) * d] = jnp.dot(
            h, w_ref[:, n * d:(n + 1) * d], preferred_element_type=F32).astype(BF16)


def _residual(y_ref, z_ref, x_ref, g_ref, ada_ref, w_ref, d):
    y = y_ref[0].astype(F32)
    z = z_ref[0].astype(F32)
    t = (y * (z * jax.nn.sigmoid(z))).astype(BF16)
    r = jnp.dot(t, w_ref[...], preferred_element_type=F32)
    ms = jnp.mean(r * r, axis=-1, keepdims=True)
    rn = r * lax.rsqrt(ms + RMS_EPS) * g_ref[...]
    gate = ada_ref[0][:, 2 * d:3 * d]
    return x_ref[0] + gate * rn


def _pre_kernel(x_ref, g_ref, ada_ref, w_ref, u_ref, *, d):
    _project(x_ref[0], g_ref, ada_ref, w_ref, u_ref, d)


def _post_kernel(y_ref, z_ref, x_ref, g_ref, ada_ref, w_ref, o_ref, *, d):
    o_ref[0] = _residual(y_ref, z_ref, x_ref, g_ref, ada_ref, w_ref, d)


def _post_pre_kernel(y_ref, z_ref, x_ref, g_post_ref, ada_ref, w_out_ref, g_pre_ref, ada_next_ref, w_in_ref,
                     o_ref, u_ref, *, d):
    x = _residual(y_ref, z_ref, x_ref, g_post_ref, ada_ref, w_out_ref, d)
    o_ref[0] = x
    _project(x, g_pre_ref, ada_next_ref, w_in_ref, u_ref, d)


def _row_specs(tm, d):
    row = pl.BlockSpec((1, tm, d), lambda i, j: (i, j, 0))
    gain = pl.BlockSpec((1, d), lambda i, j: (0, 0))
    ada = pl.BlockSpec((1, 1, 3 * d), lambda i, j: (i, 0, 0))
    return row, gain, ada


_ROW_PARAMS = dict(compiler_params=pltpu.CompilerParams(
    dimension_semantics=("arbitrary", "arbitrary"), vmem_limit_bytes=VMEM_LIMIT))


def _pre(x, g_pre, ada, w_in):
    b, s, d = x.shape
    n_out = w_in.shape[1]
    tm = ROW_TILE
    row, gain, ada_spec = _row_specs(tm, d)
    return pl.pallas_call(
        functools.partial(_pre_kernel, d=d),
        grid=(b, s // tm),
        in_specs=[row, gain, ada_spec, pl.BlockSpec((d, n_out), lambda i, j: (0, 0))],
        out_specs=pl.BlockSpec((1, tm, n_out), lambda i, j: (i, j, 0)),
        out_shape=jax.ShapeDtypeStruct((b, s, n_out), BF16),
        name="pre", **_ROW_PARAMS,
    )(x, g_pre.reshape(1, d), ada.reshape(b, 1, 3 * d), w_in)


def _post(y, u, x, g_post, ada, w_out, nxt=None):
    b, s, d = x.shape
    tm = ROW_TILE
    z_col = u.shape[2] // d - 1
    row, gain, ada_spec = _row_specs(tm, d)
    in_specs = [row, pl.BlockSpec((1, tm, d), lambda i, j: (i, j, z_col)), row, gain, ada_spec,
                pl.BlockSpec((d, d), lambda i, j: (0, 0))]
    args = [y, u, x, g_post.reshape(1, d), ada.reshape(b, 1, 3 * d), w_out]
    if nxt is None:
        return pl.pallas_call(
            functools.partial(_post_kernel, d=d),
            grid=(b, s // tm), in_specs=in_specs, out_specs=row,
            out_shape=jax.ShapeDtypeStruct((b, s, d), F32),
            name="post", **_ROW_PARAMS,
        )(*args)
    g_pre, ada_next, w_in = nxt
    n_out = w_in.shape[1]
    return pl.pallas_call(
        functools.partial(_post_pre_kernel, d=d),
        grid=(b, s // tm),
        in_specs=in_specs + [gain, ada_spec, pl.BlockSpec((d, n_out), lambda i, j: (0, 0))],
        out_specs=[row, pl.BlockSpec((1, tm, n_out), lambda i, j: (i, j, 0))],
        out_shape=[jax.ShapeDtypeStruct((b, s, d), F32), jax.ShapeDtypeStruct((b, s, n_out), BF16)],
        name="post_pre", **_ROW_PARAMS,
    )(*args, g_pre.reshape(1, d), ada_next.reshape(b, 1, 3 * d), w_in)


def _sb_kernel(q_ref, k_ref, v_ref, o_ref, vt_ref, acc_ref, *, t, n_sub):
    eye = jnp.where(lax.broadcasted_iota(jnp.int32, (LANES, LANES), 0)
                    == lax.broadcasted_iota(jnp.int32, (LANES, LANES), 1), 1.0, 0.0).astype(BF16)
    for sub in range(n_sub):
        j = pl.program_id(2) * n_sub + sub
        v_tile = v_ref[0, pl.ds(pl.multiple_of(j * t, t), t), :]
        vt_ref[j] = lax.dot_general(eye, v_tile, _NT, preferred_element_type=F32).astype(BF16)

    lane = lax.broadcasted_iota(jnp.int32, (t, LANES), 1)
    kpos = lax.broadcasted_iota(jnp.int32, (t, 2 * t), 0)
    qpos = lax.broadcasted_iota(jnp.int32, (t, 2 * t), 1) % t
    strict = kpos < qpos
    upper = jnp.where(lax.broadcasted_iota(jnp.int32, (t, t), 1)
                      > lax.broadcasted_iota(jnp.int32, (t, t), 0), 1.0, 0.0).astype(BF16)

    def logits(qs, j):
        rows = pl.ds(pl.multiple_of(j * t, t), t)
        return lax.dot_general(k_ref[0, rows, :], qs, _NT, preferred_element_type=F32)

    def soften(z, diag):
        z = jnp.minimum(z, SB_LOGIT_CAP)
        drop = jnp.log2(1.0 + jnp.exp2(z))
        log_beta = z - drop
        return (jnp.where(strict, drop, 0.0) if diag else drop), log_beta

    def suffix(drop):
        return jnp.dot(upper, drop.astype(BF16), preferred_element_type=F32)

    def weigh(j, log_beta, between, carry, diag):
        w = jnp.exp2(log_beta - between - carry)
        if diag:
            w = jnp.where(strict, w, 0.0)
        return jnp.dot(vt_ref[j], w.astype(BF16), preferred_element_type=F32)

    def live(carry):
        return jnp.min(carry) < SB_EXIT_BITS

    work = []
    for sub in range(n_sub):
        qt = pl.program_id(2) * n_sub + sub
        q = q_ref[0, sub * t:(sub + 1) * t, :].astype(F32) * ((SB_HEAD_DIM ** -0.5) * LOG2E)
        qs = jnp.concatenate([jnp.where(lane < SB_HEAD_DIM, q, 0.0).astype(BF16),
                              jnp.where(lane < SB_HEAD_DIM, 0.0, q).astype(BF16)], axis=0)
        j_prev = jnp.maximum(qt - 1, 0)
        work.append(dict(sub=sub, qt=qt, qs=qs, j_prev=j_prev,
                         z_d=logits(qs, qt), z_p=logits(qs, j_prev)))
    for w_ in work:
        w_["drop_d"], w_["lb_d"] = soften(w_["z_d"], True)
        w_["btw_d"] = suffix(w_["drop_d"])
        w_["drop_p"], w_["lb_p"] = soften(w_["z_p"], False)
        w_["btw_p"] = suffix(w_["drop_p"])
    for w_ in work:
        carry = w_["btw_d"][0:1, :] + w_["drop_d"][0:1, :]
        has_prev = jnp.full((1, 2 * t), w_["qt"], jnp.int32) >= 1
        carry = jnp.where(has_prev, carry, -NEG_INF)
        pv = (weigh(w_["qt"], w_["lb_d"], w_["btw_d"], jnp.zeros((1, 2 * t), F32), True)
              + weigh(w_["j_prev"], w_["lb_p"], w_["btw_p"], carry, False))
        acc_ref[w_["sub"]] = pv
        w_["carry"] = carry + w_["btw_p"][0:1, :] + w_["drop_p"][0:1, :]

    def cond(c):
        return jnp.logical_and(c[0] >= 0, c[1])

    def body(c):
        j_last, _, carries = c
        stepped = []
        for w_, carry in zip(work, carries):
            j = j_last - (n_sub - 1 - w_["sub"])
            carry = jnp.where(jnp.full((1, 2 * t), j, jnp.int32) >= 0, carry, -NEG_INF)
            j = jnp.maximum(j, 0)
            drop, log_beta = soften(logits(w_["qs"], j), False)
            between = suffix(drop)
            acc_ref[w_["sub"]] += weigh(j, log_beta, between, carry, False)
            stepped.append(carry + between[0:1, :] + drop[0:1, :])
        return j_last - 1, live(functools.reduce(jnp.minimum, stepped)), tuple(stepped)

    first = tuple(w_["carry"] for w_ in work)
    lax.while_loop(cond, body, (work[-1]["qt"] - 2, live(functools.reduce(jnp.minimum, first)), first))

    for w_ in work:
        acc = acc_ref[w_["sub"]]
        out_t = jnp.concatenate([acc[0:SB_HEAD_DIM, 0:t], acc[SB_HEAD_DIM:LANES, t:2 * t]], axis=0)
        o_ref[0, w_["sub"] * t:(w_["sub"] + 1) * t, :] = out_t.T.astype(o_ref.dtype)


def _sb_attention(u, d):
    b, s, _ = u.shape
    t = ATT_TILE
    n_sub = SB_Q_TILES
    npair = d // LANES
    return pl.pallas_call(
        functools.partial(_sb_kernel, t=t, n_sub=n_sub),
        grid=(b, npair, s // (t * n_sub)),
        in_specs=[
            pl.BlockSpec((1, t * n_sub, LANES), lambda i, h, j: (i, j, h)),
            pl.BlockSpec((1, s, LANES), lambda i, h, j: (i, 0, npair + h)),
            pl.BlockSpec((1, s, LANES), lambda i, h, j: (i, 0, 2 * npair + h)),
        ],
        out_specs=pl.BlockSpec((1, t * n_sub, LANES), lambda i, h, j: (i, j, h)),
        out_shape=jax.ShapeDtypeStruct((b, s, d), BF16),
        scratch_shapes=[pltpu.VMEM((s // t, LANES, t), BF16),
                        pltpu.VMEM((n_sub, LANES, 2 * t), F32)],
        compiler_params=pltpu.CompilerParams(
            dimension_semantics=("arbitrary", "arbitrary", "arbitrary"),
            vmem_limit_bytes=VMEM_LIMIT),
        name="sb_attn",
    )(u, u, u)


def _diff_kernel(q_ref, k_ref, v_ref, lam_ref, g_ref, o_ref, vt_ref, kmax_ref, bias_ref, acc_ref,
                 *, t, tq, n_heads, lam_init):
    h = pl.program_id(1)
    qi = pl.program_id(2)
    head_no = (lax.broadcasted_iota(jnp.int32, (1, 1), 0) + (h + 1)).astype(F32)
    slope = jnp.exp2(-head_no * (ALIBI_MAX_EXP / n_heads))
    lane = lax.broadcasted_iota(jnp.int32, (tq, LANES), 1)
    n_diag = tq // t
    half = lax.broadcasted_iota(jnp.int32, (SUBLANES, LANES), 0)
    feat = lax.broadcasted_iota(jnp.int32, (SUBLANES, LANES), 1)
    pick = jnp.where((feat >= DIFF_HEAD_DIM) == (half == 1), 1.0, 0.0)
    pick = jnp.where(half < 2, pick, 0.0).astype(BF16)

    def sq_norms(x):
        xf = x.astype(F32)
        return lax.dot_general(pick, (xf * xf).astype(BF16), _NT, preferred_element_type=F32)

    @pl.when(qi == 0)
    def _():
        kmax_ref[0] = jnp.zeros((SUBLANES, LANES), F32)
        kpos = lax.broadcasted_iota(jnp.int32, (t, t), 0)
        qpos = lax.broadcasted_iota(jnp.int32, (t, t), 1)
        ahead = jnp.maximum(kpos - qpos, 0).astype(F32)
        bias_ref[...] = jnp.where(kpos // CHUNK <= qpos // CHUNK, -2.0 * slope * ahead, NEG_INF)

    eye = jnp.where(lax.broadcasted_iota(jnp.int32, (LANES, LANES), 0)
                    == lax.broadcasted_iota(jnp.int32, (LANES, LANES), 1), 1.0, 0.0).astype(BF16)
    for d in range(n_diag):
        j = qi * n_diag + d
        rows = pl.ds(pl.multiple_of(j * t, t), t)
        vt = lax.dot_general(eye, v_ref[0, rows, :], _NT, preferred_element_type=F32)
        vt_ref[j, 0:LANES, :] = vt.astype(BF16)
        vt_ref[j, LANES:LANES + ROWS_ONES, :] = jnp.ones((ROWS_ONES, t), BF16)
        k_sq = jnp.max(sq_norms(k_ref[0, rows, :]), axis=1, keepdims=True)
        kmax_ref[j + 1] = jnp.maximum(kmax_ref[j], k_sq)

    sub = lax.broadcasted_iota(jnp.int32, (t, LANES), 0)
    q_raw = q_ref[0]
    q = q_raw.astype(F32) * (DIFF_HEAD_DIM ** -0.5)
    one_col = jnp.where(lane == 0, 1.0, 0.0).astype(BF16)
    q1 = jnp.concatenate([jnp.where(lane < DIFF_HEAD_DIM, q, 0.0).astype(BF16), one_col], axis=1)
    q2 = jnp.concatenate([jnp.where(lane < DIFF_HEAD_DIM, 0.0, q).astype(BF16), one_col], axis=1)
    k_aug = jnp.where(lax.broadcasted_iota(jnp.int32, (t, LANES), 1) == 0,
                      slope * sub.astype(F32), 0.0).astype(BF16)
    q_pos = lax.broadcasted_iota(jnp.int32, (1, tq), 1) + qi * tq
    q_sq = sq_norms(q_raw) * (BOUND_SLACK / DIFF_HEAD_DIM)

    def key_tile(j):
        k2 = k_ref[0, pl.ds(pl.multiple_of(j * t, t), t), :]
        return jnp.concatenate([k2, k_aug], axis=1)

    def query_bias(j):
        return -slope * (q_pos - j * t).astype(F32)

    def update(idx, scores, biases, vt, m):
        m_new = m
        for s, c in zip(scores, biases):
            m_new = jnp.maximum(m_new, jnp.max(s, axis=0, keepdims=True) + c)
        alpha = jnp.exp(m - m_new)
        p = jnp.concatenate([jnp.exp(s - (m_new - c)).astype(BF16) for s, c in zip(scores, biases)], axis=0)
        acc_ref[idx] = alpha * acc_ref[idx] + jnp.dot(vt, p, preferred_element_type=F32)
        return m_new

    def accumulate(idx, scores, biases, vt, m):
        p = jnp.concatenate([jnp.exp(s - (m - c)).astype(BF16) for s, c in zip(scores, biases)], axis=0)
        acc_ref[idx] += jnp.dot(vt, p, preferred_element_type=F32)
        return m

    def group(j_top, m1, m2, step):
        js = [jnp.maximum(j_top - g, 0) for g in range(DIFF_GROUP)]
        cs = [jnp.where(jnp.full((1, tq), j_top - g, jnp.int32) >= 0, query_bias(js[g]), NEG_INF)
              for g in range(DIFF_GROUP)]
        ks = [key_tile(j) for j in js]
        vt = jnp.concatenate([vt_ref[j] for j in js], axis=1)
        s1 = [lax.dot_general(k, q1, _NT, preferred_element_type=F32) for k in ks]
        s2 = [lax.dot_general(k, q2, _NT, preferred_element_type=F32) for k in ks]
        return step(0, s1, cs, vt, m1), step(1, s2, cs, vt, m2)

    def diagonal_scores(qm):
        scores = []
        for d in range(n_diag):
            s = lax.dot_general(key_tile(qi * n_diag + d), qm[d * t:, :], _NT, preferred_element_type=F32)
            scores.append(jnp.concatenate([s[:, 0:t] + bias_ref[...], s[:, t:]], axis=1) if d < n_diag - 1
                          else s + bias_ref[...])
        return scores

    def diagonal_softmax(idx, scores):
        shifts = [query_bias(qi * n_diag + d)[:, d * t:] for d in range(n_diag)]
        m = jnp.concatenate(
            [functools.reduce(jnp.maximum, [jnp.max(scores[d][:, (b - d) * t:(b - d + 1) * t], axis=0, keepdims=True)
                                            + shifts[d][:, (b - d) * t:(b - d + 1) * t] for d in range(b + 1)])
             for b in range(n_diag)], axis=1)
        for d in range(n_diag):
            p = jnp.exp(scores[d] - (m[:, d * t:] - shifts[d])).astype(BF16)
            pv = jnp.dot(vt_ref[qi * n_diag + d], p, preferred_element_type=F32)
            if d == 0:
                acc_ref[idx] = pv
            else:
                acc_ref[idx, :, d * t:] += pv
        return m

    def headroom(j_top, m1, m2):
        k_sq = kmax_ref[jnp.maximum(j_top, 0) + 1]
        reach = jnp.sqrt(q_sq * k_sq[:, 0:1])
        nearest = -slope * (q_pos - ((j_top + 1) * t - 1)).astype(F32)
        return jnp.max(jnp.maximum(reach[0:1] - m1, reach[1:2] - m2) + nearest)

    diag1, diag2 = diagonal_scores(q1), diagonal_scores(q2)
    m1 = diagonal_softmax(0, diag1)
    m2 = diagonal_softmax(1, diag2)

    def cond(c):
        return jnp.logical_and(c[0] >= 0, c[1] > -EXP_ZERO_BELOW)

    def body(c):
        j_top, room, m1, m2 = c
        j_next = j_top - DIFF_GROUP

        def fast():
            return (headroom(j_next, m1, m2),) + group(j_top, m1, m2, accumulate)

        def exact():
            return (headroom(j_next, m1, m2),) + group(j_top, m1, m2, update)

        room, m1, m2 = lax.cond(room < FAST_BELOW, fast, exact)
        return j_next, room, m1, m2

    j_first = qi * n_diag - 1
    lax.while_loop(cond, body, (j_first, headroom(j_first, m1, m2), m1, m2))

    lam = lam_ref[...]
    lam_full = (jnp.exp(jnp.sum(lam[0:1] * lam[1:2], keepdims=True))
                - jnp.exp(jnp.sum(lam[2:3] * lam[3:4], keepdims=True)) + lam_init)
    a1 = acc_ref[0]
    a2 = acc_ref[1]
    out_t = (a1[0:LANES] / a1[LANES:LANES + 1]
             - lam_full * (a2[0:LANES] / a2[LANES:LANES + 1]))
    out = out_t.T
    ms = jnp.mean(out * out, axis=-1, keepdims=True)
    out = out * lax.rsqrt(ms + RMS_EPS) * g_ref[...] * (1.0 - lam_init)
    o_ref[0] = out.astype(o_ref.dtype)


def _diff_attention(u, d, lam, g_sub, lam_init):
    b, s, _ = u.shape
    t = ATT_TILE
    tq = DIFF_Q_TILE
    n_heads = d // LANES
    return pl.pallas_call(
        functools.partial(_diff_kernel, t=t, tq=tq, n_heads=n_heads, lam_init=lam_init),
        grid=(b, n_heads, s // tq),
        in_specs=[
            pl.BlockSpec((1, tq, LANES), lambda i, h, j: (i, j, h)),
            pl.BlockSpec((1, s, LANES), lambda i, h, j: (i, 0, n_heads + h)),
            pl.BlockSpec((1, s, LANES), lambda i, h, j: (i, 0, 2 * n_heads + h)),
            pl.BlockSpec(lam.shape, lambda i, h, j: (0, 0)),
            pl.BlockSpec((1, LANES), lambda i, h, j: (0, 0)),
        ],
        out_specs=pl.BlockSpec((1, tq, LANES), lambda i, h, j: (i, j, h)),
        out_shape=jax.ShapeDtypeStruct((b, s, d), BF16),
        scratch_shapes=[pltpu.VMEM((s // t, LANES + ROWS_ONES, t), BF16),
                        pltpu.VMEM((s // t + 1, SUBLANES, LANES), F32),
                        pltpu.VMEM((t, t), F32),
                        pltpu.VMEM((2, LANES + ROWS_ONES, tq), F32)],
        compiler_params=pltpu.CompilerParams(
            dimension_semantics=("arbitrary", "arbitrary", "arbitrary"),
            vmem_limit_bytes=VMEM_LIMIT),
        name="diff_attn",
    )(u, u, u, lam, g_sub.reshape(1, LANES))


def kernel(x, c, w_ada, b_ada, g_pre, g_post, w_in, w_out, diff_lambda, diff_subln):
    depth = w_ada.shape[0]
    d = x.shape[-1]
    assert w_in.shape[2] == 4 * d and d % LANES == 0
    assert x.shape[1] % ROW_TILE == 0 and x.shape[1] % DIFF_Q_TILE == 0 and x.shape[1] % (ATT_TILE * SB_Q_TILES) == 0
    assert DIFF_Q_TILE % ATT_TILE == 0 and ATT_TILE % CHUNK == 0
    ada = _ada(c, w_ada, b_ada)
    w_in_b = w_in.astype(BF16)
    w_out_b = w_out.astype(BF16)
    u = _pre(x, g_pre[0], ada[0], w_in_b[0])
    for i in range(depth):
        if i % N_MIXERS == 0:
            y = _sb_attention(u, d)
        else:
            j = i // N_MIXERS
            y = _diff_attention(u, d, diff_lambda[j], diff_subln[j], _diff_lambda_init(i))
        if i + 1 < depth:
            x, u = _post(y, u, x, g_post[i], ada[i], w_out_b[i], (g_pre[i + 1], ada[i + 1], w_in_b[i + 1]))
        else:
            x = _post(y, u, x, g_post[i], ada[i], w_out_b[i])
    return x
```
